```python
import jax, jax.numpy as jnp
from jax import lax
import numpy as np

D_MODEL = 2048
BATCH = 4
SEQ = 4096
DEPTH = 1

CHUNK = 64
N_MEM = 256
EPS = 1e-6
D_MIX = D_MODEL
D_POOL = D_MIX // 2
POOL_WINDOWS = (2, 4, 8, 16)
N_POOL_GROUPS = len(POOL_WINDOWS)
POOL_GROUP_DIM = D_POOL // N_POOL_GROUPS
D_SGU = D_MIX - D_POOL
SGU_BLOCK = 128
N_SGU_HEADS = 8
SGU_HEAD_DIM = D_SGU // N_SGU_HEADS
D_IN = D_POOL + 2 * D_SGU
N_XATTN_HEADS = 4
XATTN_HEAD_DIM = D_MODEL // N_XATTN_HEADS
D_FF = ((8 * D_MODEL // 3 + 255) // 256) * 256

kernel_name = "hybrid_pool_sgu_memxattn_block"


def rmsnorm(x, g):
    x32 = x.astype(jnp.float32)
    y = x32 * lax.rsqrt(jnp.mean(x32 * x32, axis=-1, keepdims=True) + EPS)
    return (y * g.astype(jnp.float32)).astype(x.dtype)


def multiscale_pool(a, pool_w, pool_scale):
    B, S, _ = a.shape
    a32 = a.astype(jnp.float32)
    csum = jnp.cumsum(a32, axis=1)
    pos = jnp.arange(1, S + 1, dtype=jnp.float32)[None, :, None]
    outs = []
    for g, w in enumerate(POOL_WINDOWS):
        sl = slice(g * POOL_GROUP_DIM, (g + 1) * POOL_GROUP_DIM)
        c = csum[..., sl]
        prev = jnp.pad(c, ((0, 0), (w, 0), (0, 0)))[:, :S]
        mean = (c - prev) / jnp.minimum(pos, float(w))
        outs.append(mean - a32[..., sl])
    p = jnp.stack(outs, axis=2).astype(a.dtype)
    y = jnp.einsum('bsgc,gcd->bsgd', p, pool_w)
    return y.reshape(B, S, D_POOL) * pool_scale


def spatial_gating(uv, sgu_norm_g, w_spatial, b_spatial):
    B, S, _ = uv.shape
    u, v = uv[..., :D_SGU], uv[..., D_SGU:]
    v = rmsnorm(v, sgu_norm_g)
    v = v.reshape(B, S // SGU_BLOCK, SGU_BLOCK, N_SGU_HEADS, SGU_HEAD_DIM)
    t = jnp.arange(SGU_BLOCK)
    mask = (t[None, :] // CHUNK) <= (t[:, None] // CHUNK)
    ws = jnp.where(mask[None], w_spatial, 0.0)
    mixed = jnp.einsum('hts,bnshc->bnthc', ws, v)
    mixed = mixed + b_spatial.T[None, None, :, :, None]
    return u * mixed.reshape(B, S, D_SGU)


def memory_cross_attention(h, m, w_q, w_k, w_v, w_o):
    B, S, _ = h.shape
    M = m.shape[1]
    q = (h @ w_q).reshape(B, S, N_XATTN_HEADS, XATTN_HEAD_DIM)
    k = (m @ w_k).reshape(B, M, N_XATTN_HEADS, XATTN_HEAD_DIM)
    v = (m @ w_v).reshape(B, M, N_XATTN_HEADS, XATTN_HEAD_DIM)
    s = jnp.einsum('bshd,bmhd->bhsm', q, k).astype(jnp.float32) * (XATTN_HEAD_DIM ** -0.5)
    p = jax.nn.softmax(s, axis=-1).astype(v.dtype)
    o = jnp.einsum('bhsm,bmhd->bshd', p, v).reshape(B, S, D_MODEL)
    return o @ w_o


def swiglu(h, w_gate, w_up, w_down):
    return (jax.nn.silu(h @ w_gate) * (h @ w_up)) @ w_down


def setup_inputs(seed: int = 0) -> dict:
    key = jax.random.key(seed)
    ks = jax.random.split(key, 24)
    L = DEPTH
    f32 = jnp.float32

    def nrm(k, shape, scale):
        return jax.random.normal(k, shape, f32) * scale

    def gain(k, shape):
        return 1.0 + 0.02 * jax.random.normal(k, shape, f32)

    return {
        "x": jax.random.normal(ks[0], (BATCH, SEQ, D_MODEL), f32),
        "mem": jax.random.normal(ks[1], (BATCH, N_MEM, D_MODEL), f32),
        "norm_mix_g": gain(ks[2], (L, D_MODEL)),
        "w_in": nrm(ks[3], (L, D_MODEL, D_IN), D_MODEL ** -0.5),
        "pool_w": nrm(ks[4], (L, N_POOL_GROUPS, POOL_GROUP_DIM, POOL_GROUP_DIM), POOL_GROUP_DIM ** -0.5),
        "pool_scale": 1.0 + 0.1 * jax.random.normal(ks[5], (L, D_POOL), f32),
        "sgu_norm_g": gain(ks[6], (L, D_SGU)),
        "w_spatial": nrm(ks[7], (L, N_SGU_HEADS, SGU_BLOCK, SGU_BLOCK), SGU_BLOCK ** -0.5),
        "b_spatial": 1.0 + 0.1 * jax.random.normal(ks[8], (L, N_SGU_HEADS, SGU_BLOCK), f32),
        "w_out": nrm(ks[9], (L, D_MIX, D_MODEL), D_MIX ** -0.5),
        "norm_xattn_g": gain(ks[10], (L, D_MODEL)),
        "norm_mem_g": gain(ks[11], (L, D_MODEL)),
        "w_q": nrm(ks[12], (L, D_MODEL, D_MODEL), D_MODEL ** -0.5),
        "w_k": nrm(ks[13], (L, D_MODEL, D_MODEL), D_MODEL ** -0.5),
        "w_v": nrm(ks[14], (L, D_MODEL, D_MODEL), D_MODEL ** -0.5),
        "w_o": nrm(ks[15], (L, D_MODEL, D_MODEL), D_MODEL ** -0.5),
        "norm_ffn_g": gain(ks[16], (L, D_MODEL)),
        "w_gate": nrm(ks[17], (L, D_MODEL, D_FF), D_MODEL ** -0.5),
        "w_up": nrm(ks[18], (L, D_MODEL, D_FF), D_MODEL ** -0.5),
        "w_down": nrm(ks[19], (L, D_FF, D_MODEL), D_FF ** -0.5),
        "final_norm_g": gain(ks[20], (D_MODEL,)),
    }


def reference(x, mem, norm_mix_g, w_in, pool_w, pool_scale, sgu_norm_g, w_spatial,
              b_spatial, w_out, norm_xattn_g, norm_mem_g, w_q, w_k, w_v, w_o,
              norm_ffn_g, w_gate, w_up, w_down, final_norm_g):
    for l in range(DEPTH):
        h = rmsnorm(x, norm_mix_g[l])
        proj = h @ w_in[l]
        y_pool = multiscale_pool(proj[..., :D_POOL], pool_w[l], pool_scale[l])
        y_sgu = spatial_gating(proj[..., D_POOL:], sgu_norm_g[l], w_spatial[l], b_spatial[l])
        x = x + jnp.concatenate([y_pool, y_sgu], axis=-1) @ w_out[l]
        h = rmsnorm(x, norm_xattn_g[l])
        m = rmsnorm(mem, norm_mem_g[l])
        x = x + memory_cross_attention(h, m, w_q[l], w_k[l], w_v[l], w_o[l])
        h = rmsnorm(x, norm_ffn_g[l])
        x = x + swiglu(h, w_gate[l], w_up[l], w_down[l])
    return rmsnorm(x, final_norm_g)
```

```python
import functools

import jax
import jax.numpy as jnp
from jax import lax
from jax.experimental import pallas as pl
from jax.experimental.pallas import tpu as pltpu

EPS = 1e-6
CHUNK = 64
POOL_WINDOWS = (2, 4, 8, 16)
SGU_BLOCK = 128
N_SGU_HEADS = 8
N_XATTN_HEADS = 4

LANES = 128
HALO = 16
VMEM_LIMIT = 56 * 1024 * 1024

BF16 = jnp.bfloat16
F32 = jnp.float32


def _rms(x, g):
    ms = jnp.mean(x * x, axis=-1, keepdims=True)
    return x * lax.rsqrt(ms + EPS) * g


def _dot(a, b):
    return jnp.dot(a, b, preferred_element_type=F32)


def _resident(shape):
    return pl.BlockSpec(shape, lambda *_: (0,) * len(shape), pipeline_mode=pl.Buffered(1))


def _kv_kernel(mem_ref, g_ref, wk_ref, wv_ref, k_ref, v_ref):
    m = _rms(mem_ref[...], g_ref[...]).astype(BF16)
    k_ref[...] = _dot(m, wk_ref[...]).astype(BF16)
    v_ref[...] = _dot(m, wv_ref[...]).astype(BF16)


def _kv_proj(mem2, g, wk, wv, *, tn=512):
    rows, d = mem2.shape
    return pl.pallas_call(
        _kv_kernel,
        grid=(d // tn,),
        in_specs=[
            pl.BlockSpec((rows, d), lambda j: (0, 0)),
            pl.BlockSpec((1, d), lambda j: (0, 0)),
            pl.BlockSpec((d, tn), lambda j: (0, j)),
            pl.BlockSpec((d, tn), lambda j: (0, j)),
        ],
        out_specs=[
            pl.BlockSpec((rows, tn), lambda j: (0, j)),
            pl.BlockSpec((rows, tn), lambda j: (0, j)),
        ],
        out_shape=[jax.ShapeDtypeStruct((rows, d), BF16)] * 2,
        compiler_params=pltpu.CompilerParams(
            dimension_semantics=("arbitrary",), vmem_limit_bytes=VMEM_LIMIT),
        name="kv_proj",
    )(mem2, g, wk, wv)


def _mix_kernel(x_ref, g_ref, w_in_ref, pool_w_ref, pool_scale_ref, sgu_g_ref, ws_ref, bias_ref,
                w_out_ref, o_ref, halo_ref, wsm_ref, y_ref, *, tm, tiles_per_seq, d_pool, d_sgu):
    i = pl.program_id(0)
    seq_tile = i % tiles_per_seq
    group = d_pool // len(POOL_WINDOWS)
    head = d_sgu // N_SGU_HEADS

    @pl.when(i == 0)
    def _():
        t = lax.broadcasted_iota(jnp.int32, (SGU_BLOCK, SGU_BLOCK), 0)
        s = lax.broadcasted_iota(jnp.int32, (SGU_BLOCK, SGU_BLOCK), 1)
        mask = (s // CHUNK) <= (t // CHUNK)
        for h in range(N_SGU_HEADS):
            wsm_ref[h] = jnp.where(mask, ws_ref[h], 0.0).astype(BF16)

    @pl.when(seq_tile == 0)
    def _():
        halo_ref[...] = jnp.zeros_like(halo_ref)

    x = x_ref[...]
    xn = _rms(x, g_ref[...]).astype(BF16)
    a = _dot(xn, w_in_ref[:, 0:d_pool])
    u = _dot(xn, w_in_ref[:, d_pool:d_pool + d_sgu])
    v = _dot(xn, w_in_ref[:, d_pool + d_sgu:d_pool + 2 * d_sgu])

    ext = jnp.concatenate([halo_ref[...], a], axis=0)
    halo_ref[...] = a[tm - HALO:, :]
    pos = (seq_tile * tm + lax.broadcasted_iota(jnp.int32, (tm, 1), 0) + 1).astype(F32)
    for gi, w in enumerate(POOL_WINDOWS):
        cols = slice(gi * group, (gi + 1) * group)
        s = ext[:, cols]
        k = 1
        while k < w:
            s = s + pltpu.roll(s, k, axis=0)
            k *= 2
        inv = 1.0 / jnp.minimum(pos, float(w))
        p = (s[HALO:] * inv - a[:, cols]).astype(BF16)
        y_ref[:, cols] = (_dot(p, pool_w_ref[gi]) * pool_scale_ref[:, cols]).astype(BF16)

    vn = _rms(v, sgu_g_ref[...]).astype(BF16)
    for blk in range(tm // SGU_BLOCK):
        rows = slice(blk * SGU_BLOCK, (blk + 1) * SGU_BLOCK)
        for h in range(N_SGU_HEADS):
            cols = slice(h * head, (h + 1) * head)
            mixed = _dot(wsm_ref[h], vn[rows, cols]) + bias_ref[:, cols]
            y_ref[rows, d_pool + h * head:d_pool + (h + 1) * head] = (u[rows, cols] * mixed).astype(BF16)

    o_ref[...] = x + _dot(y_ref[...], w_out_ref[...])


def _mix_layer(x2, g, w_in, pool_w, pool_scale, sgu_g, ws, bias, w_out, *, seq, tm=256):
    rows, d = x2.shape
    d_in = w_in.shape[1]
    d_mix = w_out.shape[0]
    d_pool = pool_scale.shape[1]
    d_sgu = d_mix - d_pool
    group = d_pool // len(POOL_WINDOWS)
    kern = functools.partial(_mix_kernel, tm=tm, tiles_per_seq=seq // tm, d_pool=d_pool, d_sgu=d_sgu)
    return pl.pallas_call(
        kern,
        grid=(rows // tm,),
        in_specs=[
            pl.BlockSpec((tm, d), lambda i: (i, 0)),
            _resident((1, d)),
            _resident((d, d_in)),
            _resident((len(POOL_WINDOWS), group, group)),
            _resident((1, d_pool)),
            _resident((1, d_sgu)),
            _resident((N_SGU_HEADS, SGU_BLOCK, SGU_BLOCK)),
            _resident((SGU_BLOCK, d_sgu)),
            _resident((d_mix, d)),
        ],
        out_specs=pl.BlockSpec((tm, d), lambda i: (i, 0)),
        out_shape=jax.ShapeDtypeStruct((rows, d), F32),
        scratch_shapes=[
            pltpu.VMEM((HALO, d_pool), F32),
            pltpu.VMEM((N_SGU_HEADS, SGU_BLOCK, SGU_BLOCK), BF16),
            pltpu.VMEM((tm, d_mix), BF16),
        ],
        compiler_params=pltpu.CompilerParams(
            dimension_semantics=("arbitrary",), vmem_limit_bytes=VMEM_LIMIT),
        name="mix_layer",
    )(x2, g, w_in, pool_w, pool_scale, sgu_g, ws, bias, w_out)


def _attn_kernel(x_ref, g_ref, wq_ref, k_ref, v_ref, wo_ref, o_ref, oh_ref, *, head_dim):
    x = x_ref[...]
    h = _rms(x, g_ref[...]).astype(BF16)
    q = _dot(h, wq_ref[...]).astype(BF16)
    scale = head_dim ** -0.5
    for hd in range(N_XATTN_HEADS):
        cols = slice(hd * head_dim, (hd + 1) * head_dim)
        s = lax.dot_general(q[:, cols], k_ref[:, cols], (((1,), (1,)), ((), ())),
                            preferred_element_type=F32) * scale
        e = jnp.exp(s - jnp.max(s, axis=-1, keepdims=True))
        p = (e * (1.0 / jnp.sum(e, axis=-1, keepdims=True))).astype(BF16)
        oh_ref[:, cols] = _dot(p, v_ref[:, cols]).astype(BF16)
    o_ref[...] = x + _dot(oh_ref[...], wo_ref[...])


def _attn_layer(x2, g, wq, k, v, wo, *, seq, n_mem, tm=256):
    rows, d = x2.shape
    tiles_per_seq = seq // tm
    kern = functools.partial(_attn_kernel, head_dim=d // N_XATTN_HEADS)
    return pl.pallas_call(
        kern,
        grid=(rows // tm,),
        in_specs=[
            pl.BlockSpec((tm, d), lambda i: (i, 0)),
            _resident((1, d)),
            _resident((d, d)),
            pl.BlockSpec((n_mem, d), lambda i: (i // tiles_per_seq, 0)),
            pl.BlockSpec((n_mem, d), lambda i: (i // tiles_per_seq, 0)),
            _resident((d, d)),
        ],
        out_specs=pl.BlockSpec((tm, d), lambda i: (i, 0)),
        out_shape=jax.ShapeDtypeStruct((rows, d), F32),
        scratch_shapes=[pltpu.VMEM((tm, d), BF16)],
        compiler_params=pltpu.CompilerParams(
            dimension_semantics=("arbitrary",), vmem_limit_bytes=VMEM_LIMIT),
        name="attn_layer",
    )(x2, g, wq, k, v, wo)


def _ffn_kernel(x_ref, g_ref, wg_ref, wu_ref, wd_ref, fg_ref, o_ref, h_ref, acc_ref, *, final_norm):
    j = pl.program_id(1)

    @pl.when(j == 0)
    def _():
        x = x_ref[...]
        h_ref[...] = _rms(x, g_ref[...]).astype(BF16)
        acc_ref[...] = x

    h = h_ref[...]
    gate = _dot(h, wg_ref[...])
    up = _dot(h, wu_ref[...])
    act = (gate * jax.nn.sigmoid(gate) * up).astype(BF16)
    acc_ref[...] += _dot(act, wd_ref[...])

    @pl.when(j == pl.num_programs(1) - 1)
    def _():
        y = acc_ref[...]
        o_ref[...] = _rms(y, fg_ref[...]) if final_norm else y


def _ffn_layer(x2, g, wg, wu, wd, fg, *, final_norm, tm=512, tf=512):
    rows, d = x2.shape
    d_ff = wg.shape[1]
    kern = functools.partial(_ffn_kernel, final_norm=final_norm)
    return pl.pallas_call(
        kern,
        grid=(rows // tm, d_ff // tf),
        in_specs=[
            pl.BlockSpec((tm, d), lambda i, j: (i, 0)),
            _resident((1, d)),
            pl.BlockSpec((d, tf), lambda i, j: (0, j)),
            pl.BlockSpec((d, tf), lambda i, j: (0, j)),
            pl.BlockSpec((tf, d), lambda i, j: (j, 0)),
            _resident((1, d)),
        ],
        out_specs=pl.BlockSpec((tm, d), lambda i, j: (i, 0)),
        out_shape=jax.ShapeDtypeStruct((rows, d), F32),
        scratch_shapes=[pltpu.VMEM((tm, d), BF16), pltpu.VMEM((tm, d), F32)],
        compiler_params=pltpu.CompilerParams(
            dimension_semantics=("arbitrary", "arbitrary"), vmem_limit_bytes=VMEM_LIMIT),
        name="ffn_layer",
    )(x2, g, wg, wu, wd, fg)


def kernel(x, mem, norm_mix_g, w_in, pool_w, pool_scale, sgu_norm_g, w_spatial, b_spatial, w_out,
           norm_xattn_g, norm_mem_g, w_q, w_k, w_v, w_o, norm_ffn_g, w_gate, w_up, w_down, final_norm_g):
    batch, seq, d = x.shape
    n_mem = mem.shape[1]
    depth = w_in.shape[0]
    assert depth >= 1, "the final norm is fused into the last layer"
    d_sgu = sgu_norm_g.shape[1]
    head = d_sgu // N_SGU_HEADS

    x2 = x.reshape(batch * seq, d)
    mem2 = mem.reshape(batch * n_mem, d)
    row = lambda a: a.reshape(1, -1)
    for l in range(depth):
        bias = jnp.repeat(b_spatial[l].T, head, axis=1)
        x2 = _mix_layer(x2, row(norm_mix_g[l]), w_in[l].astype(BF16), pool_w[l].astype(BF16),
                        row(pool_scale[l]), row(sgu_norm_g[l]), w_spatial[l], bias,
                        w_out[l].astype(BF16), seq=seq)
        k, v = _kv_proj(mem2, row(norm_mem_g[l]), w_k[l].astype(BF16), w_v[l].astype(BF16))
        x2 = _attn_layer(x2, row(norm_xattn_g[l]), w_q[l].astype(BF16), k, v, w_o[l].astype(BF16),
                         seq=seq, n_mem=n_mem)
        x2 = _ffn_layer(x2, row(norm_ffn_g[l]), w_gate[l].astype(BF16), w_up[l].astype(BF16),
                        w_down[l].astype(BF16), row(final_norm_g), final_norm=(l == depth - 1))
    return x2.reshape(batch, seq, d)
```

```python
import functools

import jax
import jax.numpy as jnp
from jax import lax
from jax.experimental import pallas as pl
from jax.experimental.pallas import tpu as pltpu

EPS = 1e-6
CHUNK = 64
POOL_WINDOWS = (2, 4, 8, 16)
SGU_BLOCK = 128
N_SGU_HEADS = 8
N_XATTN_HEADS = 4

LANES = 128
HALO = 16
VMEM_LIMIT = 56 * 1024 * 1024

BF16 = jnp.bfloat16
F32 = jnp.float32


def _rms(x, g):
    ms = jnp.mean(x * x, axis=-1, keepdims=True)
    return x * lax.rsqrt(ms + EPS) * g


def _dot(a, b):
    return jnp.dot(a, b, preferred_element_type=F32)


def _resident(shape):
    return pl.BlockSpec(shape, lambda *_: (0,) * len(shape), pipeline_mode=pl.Buffered(1))


def _kv_kernel(mem_ref, g_ref, wk_ref, wv_ref, k_ref, v_ref):
    m = _rms(mem_ref[...], g_ref[...]).astype(BF16)
    k_ref[...] = _dot(m, wk_ref[...]).astype(BF16)
    v_ref[...] = _dot(m, wv_ref[...]).astype(BF16)


def _kv_proj(mem2, g, wk, wv, *, tn=512):
    rows, d = mem2.shape
    return pl.pallas_call(
        _kv_kernel,
        grid=(d // tn,),
        in_specs=[
            pl.BlockSpec((rows, d), lambda j: (0, 0)),
            pl.BlockSpec((1, d), lambda j: (0, 0)),
            pl.BlockSpec((d, tn), lambda j: (0, j)),
            pl.BlockSpec((d, tn), lambda j: (0, j)),
        ],
        out_specs=[
            pl.BlockSpec((rows, tn), lambda j: (0, j)),
            pl.BlockSpec((rows, tn), lambda j: (0, j)),
        ],
        out_shape=[jax.ShapeDtypeStruct((rows, d), BF16)] * 2,
        compiler_params=pltpu.CompilerParams(
            dimension_semantics=("arbitrary",), vmem_limit_bytes=VMEM_LIMIT),
        name="kv_proj",
    )(mem2, g, wk, wv)


def _mix_kernel(x_ref, g_ref, w_in_ref, pool_w_ref, pool_scale_ref, sgu_g_ref, ws_ref, bias_ref,
                w_out_ref, o_ref, halo_ref, wsm_ref, y_ref, *, tm, tiles_per_seq, d_pool, d_sgu):
    i = pl.program_id(0)
    seq_tile = i % tiles_per_seq
    group = d_pool // len(POOL_WINDOWS)
    head = d_sgu // N_SGU_HEADS

    @pl.when(i == 0)
    def _():
        t = lax.broadcasted_iota(jnp.int32, (SGU_BLOCK, SGU_BLOCK), 0)
        s = lax.broadcasted_iota(jnp.int32, (SGU_BLOCK, SGU_BLOCK), 1)
        mask = (s // CHUNK) <= (t // CHUNK)
        for h in range(N_SGU_HEADS):
            wsm_ref[h] = jnp.where(mask, ws_ref[h], 0.0).astype(BF16)

    @pl.when(seq_tile == 0)
    def _():
        halo_ref[...] = jnp.zeros_like(halo_ref)

    x = x_ref[...]
    xn = _rms(x, g_ref[...]).astype(BF16)
    a = _dot(xn, w_in_ref[:, 0:d_pool])
    u = _dot(xn, w_in_ref[:, d_pool:d_pool + d_sgu])
    v = _dot(xn, w_in_ref[:, d_pool + d_sgu:d_pool + 2 * d_sgu])

    ext = jnp.concatenate([halo_ref[...], a], axis=0)
    halo_ref[...] = a[tm - HALO:, :]
    pos = (seq_tile * tm + lax.broadcasted_iota(jnp.int32, (tm, 1), 0) + 1).astype(F32)
    for gi, w in enumerate(POOL_WINDOWS):
        cols = slice(gi * group, (gi + 1) * group)
        s = ext[:, cols]
        k = 1
        while k < w:
            s = s + pltpu.roll(s, k, axis=0)
            k *= 2
        inv = 1.0 / jnp.minimum(pos, float(w))
        p = (s[HALO:] * inv - a[:, cols]).astype(BF16)
        y_ref[:, cols] = (_dot(p, pool_w_ref[gi]) * pool_scale_ref[:, cols]).astype(BF16)

    vn = _rms(v, sgu_g_ref[...]).astype(BF16)
    for blk in range(tm // SGU_BLOCK):
        rows = slice(blk * SGU_BLOCK, (blk + 1) * SGU_BLOCK)
        for h in range(N_SGU_HEADS):
            cols = slice(h * head, (h + 1) * head)
            mixed = _dot(wsm_ref[h], vn[rows, cols]) + bias_ref[:, cols]
            y_ref[rows, d_pool + h * head:d_pool + (h + 1) * head] = (u[rows, cols] * mixed).astype(BF16)

    o_ref[...] = x + _dot(y_ref[...], w_out_ref[...])


def _mix_layer(x2, g, w_in, pool_w, pool_scale, sgu_g, ws, bias, w_out, *, seq, tm=512):
    rows, d = x2.shape
    d_in = w_in.shape[1]
    d_mix = w_out.shape[0]
    d_pool = pool_scale.shape[1]
    d_sgu = d_mix - d_pool
    group = d_pool // len(POOL_WINDOWS)
    kern = functools.partial(_mix_kernel, tm=tm, tiles_per_seq=seq // tm, d_pool=d_pool, d_sgu=d_sgu)
    return pl.pallas_call(
        kern,
        grid=(rows // tm,),
        in_specs=[
            pl.BlockSpec((tm, d), lambda i: (i, 0)),
            _resident((1, d)),
            _resident((d, d_in)),
            _resident((len(POOL_WINDOWS), group, group)),
            _resident((1, d_pool)),
            _resident((1, d_sgu)),
            _resident((N_SGU_HEADS, SGU_BLOCK, SGU_BLOCK)),
            _resident((SGU_BLOCK, d_sgu)),
            _resident((d_mix, d)),
        ],
        out_specs=pl.BlockSpec((tm, d), lambda i: (i, 0)),
        out_shape=jax.ShapeDtypeStruct((rows, d), F32),
        scratch_shapes=[
            pltpu.VMEM((HALO, d_pool), F32),
            pltpu.VMEM((N_SGU_HEADS, SGU_BLOCK, SGU_BLOCK), BF16),
            pltpu.VMEM((tm, d_mix), BF16),
        ],
        compiler_params=pltpu.CompilerParams(
            dimension_semantics=("arbitrary",), vmem_limit_bytes=VMEM_LIMIT),
        name="mix_layer",
    )(x2, g, w_in, pool_w, pool_scale, sgu_g, ws, bias, w_out)


def _attn_kernel(x_ref, g_ref, wq_ref, k_ref, v_ref, wo_ref, o_ref, oh_ref, *, head_dim):
    x = x_ref[...]
    h = _rms(x, g_ref[...]).astype(BF16)
    q = _dot(h, wq_ref[...]).astype(BF16)
    scale = head_dim ** -0.5
    for hd in range(N_XATTN_HEADS):
        cols = slice(hd * head_dim, (hd + 1) * head_dim)
        s = lax.dot_general(q[:, cols], k_ref[:, cols], (((1,), (1,)), ((), ())),
                            preferred_element_type=F32) * scale
        e = jnp.exp(s - jnp.max(s, axis=-1, keepdims=True))
        p = (e * (1.0 / jnp.sum(e, axis=-1, keepdims=True))).astype(BF16)
        oh_ref[:, cols] = _dot(p, v_ref[:, cols]).astype(BF16)
    o_ref[...] = x + _dot(oh_ref[...], wo_ref[...])


def _attn_layer(x2, g, wq, k, v, wo, *, seq, n_mem, tm=512):
    rows, d = x2.shape
    tiles_per_seq = seq // tm
    kern = functools.partial(_attn_kernel, head_dim=d // N_XATTN_HEADS)
    return pl.pallas_call(
        kern,
        grid=(rows // tm,),
        in_specs=[
            pl.BlockSpec((tm, d), lambda i: (i, 0)),
            _resident((1, d)),
            _resident((d, d)),
            pl.BlockSpec((n_mem, d), lambda i: (i // tiles_per_seq, 0)),
            pl.BlockSpec((n_mem, d), lambda i: (i // tiles_per_seq, 0)),
            _resident((d, d)),
        ],
        out_specs=pl.BlockSpec((tm, d), lambda i: (i, 0)),
        out_shape=jax.ShapeDtypeStruct((rows, d), F32),
        scratch_shapes=[pltpu.VMEM((tm, d), BF16)],
        compiler_params=pltpu.CompilerParams(
            dimension_semantics=("arbitrary",), vmem_limit_bytes=VMEM_LIMIT),
        name="attn_layer",
    )(x2, g, wq, k, v, wo)


def _ffn_kernel(x_ref, g_ref, wg_ref, wu_ref, wd_ref, fg_ref, o_ref, h_ref, *, final_norm):
    j = pl.program_id(1)

    @pl.when(j == 0)
    def _():
        x = x_ref[...]
        h_ref[...] = _rms(x, g_ref[...]).astype(BF16)
        o_ref[...] = x

    h = h_ref[...]
    gate = _dot(h, wg_ref[...])
    up = _dot(h, wu_ref[...])
    act = (gate * jax.nn.sigmoid(gate) * up).astype(BF16)
    o_ref[...] += _dot(act, wd_ref[...])

    if final_norm:
        @pl.when(j == pl.num_programs(1) - 1)
        def _():
            o_ref[...] = _rms(o_ref[...], fg_ref[...])


def _ffn_layer(x2, g, wg, wu, wd, fg, *, final_norm, tm=1024, tf=512):
    rows, d = x2.shape
    d_ff = wg.shape[1]
    kern = functools.partial(_ffn_kernel, final_norm=final_norm)
    return pl.pallas_call(
        kern,
        grid=(rows // tm, d_ff // tf),
        in_specs=[
            pl.BlockSpec((tm, d), lambda i, j: (i, 0)),
            _resident((1, d)),
            pl.BlockSpec((d, tf), lambda i, j: (0, j)),
            pl.BlockSpec((d, tf), lambda i, j: (0, j)),
            pl.BlockSpec((tf, d), lambda i, j: (j, 0)),
            _resident((1, d)),
        ],
        out_specs=pl.BlockSpec((tm, d), lambda i, j: (i, 0)),
        out_shape=jax.ShapeDtypeStruct((rows, d), F32),
        scratch_shapes=[pltpu.VMEM((tm, d), BF16)],
        compiler_params=pltpu.CompilerParams(
            dimension_semantics=("arbitrary", "arbitrary"), vmem_limit_bytes=VMEM_LIMIT),
        name="ffn_layer",
    )(x2, g, wg, wu, wd, fg)


def kernel(x, mem, norm_mix_g, w_in, pool_w, pool_scale, sgu_norm_g, w_spatial, b_spatial, w_out,
           norm_xattn_g, norm_mem_g, w_q, w_k, w_v, w_o, norm_ffn_g, w_gate, w_up, w_down, final_norm_g):
    batch, seq, d = x.shape
    n_mem = mem.shape[1]
    depth = w_in.shape[0]
    assert depth >= 1, "the final norm is fused into the last layer"
    d_sgu = sgu_norm_g.shape[1]
    head = d_sgu // N_SGU_HEADS

    x2 = x.reshape(batch * seq, d)
    mem2 = mem.reshape(batch * n_mem, d)
    row = lambda a: a.reshape(1, -1)
    for l in range(depth):
        bias = jnp.repeat(b_spatial[l].T, head, axis=1)
        x2 = _mix_layer(x2, row(norm_mix_g[l]), w_in[l].astype(BF16), pool_w[l].astype(BF16),
                        row(pool_scale[l]), row(sgu_norm_g[l]), w_spatial[l], bias,
                        w_out[l].astype(BF16), seq=seq)
        k, v = _kv_proj(mem2, row(norm_mem_g[l]), w_k[l].astype(BF16), w_v[l].astype(BF16))
        x2 = _attn_layer(x2, row(norm_xattn_g[l]), w_q[l].astype(BF16), k, v, w_o[l].astype(BF16),
                         seq=seq, n_mem=n_mem)
        x2 = _ffn_layer(x2, row(norm_ffn_g[l]), w_gate[l].astype(BF16), w_up[l].astype(BF16),
                        w_down[l].astype(BF16), row(final_norm_g), final_norm=(l == depth - 1))
    return x2.reshape(batch, seq, d)
```

```python
import functools

import jax
import jax.numpy as jnp
from jax import lax
from jax.experimental import pallas as pl
from jax.experimental.pallas import tpu as pltpu

EPS = 1e-6
CHUNK = 64
POOL_WINDOWS = (2, 4, 8, 16)
SGU_BLOCK = 128
N_SGU_HEADS = 8
N_XATTN_HEADS = 4

LANES = 128
HALO = 16
VMEM_LIMIT = 56 * 1024 * 1024

BF16 = jnp.bfloat16
F32 = jnp.float32


def _rms(x, g):
    ms = jnp.mean(x * x, axis=-1, keepdims=True)
    return x * lax.rsqrt(ms + EPS) * g


def _dot(a, b):
    return jnp.dot(a, b, preferred_element_type=F32)


def _resident(shape):
    return pl.BlockSpec(shape, lambda *_: (0,) * len(shape), pipeline_mode=pl.Buffered(1))


def _kv_kernel(mem_ref, g_ref, wk_ref, wv_ref, wq_ref, wo_ref, qk_ref, vo_ref, m_ref, *, batch, n_mem):
    @pl.when(pl.program_id(0) == 0)
    def _():
        m_ref[...] = _rms(mem_ref[...], g_ref[...]).astype(BF16)

    m = m_ref[...]
    k = _dot(m, wk_ref[...]).astype(BF16)
    v = _dot(m, wv_ref[...]).astype(BF16)
    scale = wq_ref.shape[1] ** -0.5
    for b in range(batch):
        rows = slice(b * n_mem, (b + 1) * n_mem)
        qk = lax.dot_general(wq_ref[...], k[rows], (((1,), (1,)), ((), ())), preferred_element_type=F32)
        qk_ref[b] = (qk * scale).astype(BF16)
        vo_ref[b] = _dot(v[rows], wo_ref[...]).astype(BF16)


def _kv_proj(mem2, g, wk, wv, wq, wo, *, batch, n_mem):
    rows, d = mem2.shape
    hd = d // N_XATTN_HEADS
    kern = functools.partial(_kv_kernel, batch=batch, n_mem=n_mem)
    return pl.pallas_call(
        kern,
        grid=(N_XATTN_HEADS,),
        in_specs=[
            _resident((rows, d)),
            _resident((1, d)),
            pl.BlockSpec((d, hd), lambda h: (0, h)),
            pl.BlockSpec((d, hd), lambda h: (0, h)),
            pl.BlockSpec((d, hd), lambda h: (0, h)),
            pl.BlockSpec((hd, d), lambda h: (h, 0)),
        ],
        out_specs=[
            pl.BlockSpec((batch, d, n_mem), lambda h: (0, 0, h)),
            pl.BlockSpec((batch, n_mem, d), lambda h: (0, h, 0)),
        ],
        out_shape=[
            jax.ShapeDtypeStruct((batch, d, N_XATTN_HEADS * n_mem), BF16),
            jax.ShapeDtypeStruct((batch, N_XATTN_HEADS * n_mem, d), BF16),
        ],
        scratch_shapes=[pltpu.VMEM((rows, d), BF16)],
        compiler_params=pltpu.CompilerParams(
            dimension_semantics=("arbitrary",), vmem_limit_bytes=VMEM_LIMIT),
        name="kv_proj",
    )(mem2, g, wk, wv, wq, wo)


def _mix_kernel(x_ref, g_ref, w_in_ref, pool_w_ref, pool_scale_ref, sgu_g_ref, ws_ref, bias_ref,
                w_out_ref, o_ref, halo_ref, wsm_ref, y_ref, *, tm, tiles_per_seq, d_pool, d_sgu):
    i = pl.program_id(0)
    seq_tile = i % tiles_per_seq
    group = d_pool // len(POOL_WINDOWS)
    head = d_sgu // N_SGU_HEADS

    @pl.when(i == 0)
    def _():
        t = lax.broadcasted_iota(jnp.int32, (SGU_BLOCK, SGU_BLOCK), 0)
        s = lax.broadcasted_iota(jnp.int32, (SGU_BLOCK, SGU_BLOCK), 1)
        mask = (s // CHUNK) <= (t // CHUNK)
        for h in range(N_SGU_HEADS):
            wsm_ref[h] = jnp.where(mask, ws_ref[h], 0.0).astype(BF16)

    @pl.when(seq_tile == 0)
    def _():
        halo_ref[...] = jnp.zeros_like(halo_ref)

    x = x_ref[...]
    xn = _rms(x, g_ref[...]).astype(BF16)
    a = _dot(xn, w_in_ref[:, 0:d_pool])
    u = _dot(xn, w_in_ref[:, d_pool:d_pool + d_sgu])
    v = _dot(xn, w_in_ref[:, d_pool + d_sgu:d_pool + 2 * d_sgu])

    ext = jnp.concatenate([halo_ref[...], a], axis=0)
    halo_ref[...] = a[tm - HALO:, :]
    pos = (seq_tile * tm + lax.broadcasted_iota(jnp.int32, (tm, 1), 0) + 1).astype(F32)
    for gi, w in enumerate(POOL_WINDOWS):
        cols = slice(gi * group, (gi + 1) * group)
        s = ext[:, cols]
        k = 1
        while k < w:
            s = s + pltpu.roll(s, k, axis=0)
            k *= 2
        inv = 1.0 / jnp.minimum(pos, float(w))
        p = (s[HALO:] * inv - a[:, cols]).astype(BF16)
        y_ref[:, cols] = (_dot(p, pool_w_ref[gi]) * pool_scale_ref[:, cols]).astype(BF16)

    vn = _rms(v, sgu_g_ref[...]).astype(BF16)
    for blk in range(tm // SGU_BLOCK):
        rows = slice(blk * SGU_BLOCK, (blk + 1) * SGU_BLOCK)
        for h in range(N_SGU_HEADS):
            cols = slice(h * head, (h + 1) * head)
            mixed = _dot(wsm_ref[h], vn[rows, cols]) + bias_ref[:, cols]
            y_ref[rows, d_pool + h * head:d_pool + (h + 1) * head] = (u[rows, cols] * mixed).astype(BF16)

    o_ref[...] = x + _dot(y_ref[...], w_out_ref[...])


def _mix_layer(x2, g, w_in, pool_w, pool_scale, sgu_g, ws, bias, w_out, *, seq, tm=512):
    rows, d = x2.shape
    d_in = w_in.shape[1]
    d_mix = w_out.shape[0]
    d_pool = pool_scale.shape[1]
    d_sgu = d_mix - d_pool
    group = d_pool // len(POOL_WINDOWS)
    kern = functools.partial(_mix_kernel, tm=tm, tiles_per_seq=seq // tm, d_pool=d_pool, d_sgu=d_sgu)
    return pl.pallas_call(
        kern,
        grid=(rows // tm,),
        in_specs=[
            pl.BlockSpec((tm, d), lambda i: (i, 0)),
            _resident((1, d)),
            _resident((d, d_in)),
            _resident((len(POOL_WINDOWS), group, group)),
            _resident((1, d_pool)),
            _resident((1, d_sgu)),
            _resident((N_SGU_HEADS, SGU_BLOCK, SGU_BLOCK)),
            _resident((SGU_BLOCK, d_sgu)),
            _resident((d_mix, d)),
        ],
        out_specs=pl.BlockSpec((tm, d), lambda i: (i, 0)),
        out_shape=jax.ShapeDtypeStruct((rows, d), F32),
        scratch_shapes=[
            pltpu.VMEM((HALO, d_pool), F32),
            pltpu.VMEM((N_SGU_HEADS, SGU_BLOCK, SGU_BLOCK), BF16),
            pltpu.VMEM((tm, d_mix), BF16),
        ],
        compiler_params=pltpu.CompilerParams(
            dimension_semantics=("arbitrary",), vmem_limit_bytes=VMEM_LIMIT),
        name="mix_layer",
    )(x2, g, w_in, pool_w, pool_scale, sgu_g, ws, bias, w_out)


def _attn_kernel(x_ref, g_ref, qk_ref, vo_ref, o_ref, *, n_mem):
    x = x_ref[...]
    h = _rms(x, g_ref[...]).astype(BF16)
    s = _dot(h, qk_ref[0])
    ps = []
    for hd in range(N_XATTN_HEADS):
        sh = s[:, hd * n_mem:(hd + 1) * n_mem]
        e = jnp.exp(sh - jnp.max(sh, axis=-1, keepdims=True))
        ps.append((e * (1.0 / jnp.sum(e, axis=-1, keepdims=True))).astype(BF16))
    p = jnp.concatenate(ps, axis=1)
    o_ref[...] = x + _dot(p, vo_ref[0])


def _attn_layer(x2, g, qk, vo, *, seq, tm=512):
    rows, d = x2.shape
    tiles_per_seq = seq // tm
    hm = qk.shape[2]
    kern = functools.partial(_attn_kernel, n_mem=hm // N_XATTN_HEADS)
    return pl.pallas_call(
        kern,
        grid=(rows // tm,),
        in_specs=[
            pl.BlockSpec((tm, d), lambda i: (i, 0)),
            _resident((1, d)),
            pl.BlockSpec((1, d, hm), lambda i: (i // tiles_per_seq, 0, 0)),
            pl.BlockSpec((1, hm, d), lambda i: (i // tiles_per_seq, 0, 0)),
        ],
        out_specs=pl.BlockSpec((tm, d), lambda i: (i, 0)),
        out_shape=jax.ShapeDtypeStruct((rows, d), F32),
        compiler_params=pltpu.CompilerParams(
            dimension_semantics=("arbitrary",), vmem_limit_bytes=VMEM_LIMIT),
        name="attn_layer",
    )(x2, g, qk, vo)


def _ffn_kernel(x_ref, g_ref, wg_ref, wu_ref, wd_ref, fg_ref, o_ref, h_ref, *, final_norm):
    j = pl.program_id(1)

    @pl.when(j == 0)
    def _():
        x = x_ref[...]
        h_ref[...] = _rms(x, g_ref[...]).astype(BF16)
        o_ref[...] = x

    h = h_ref[...]
    gate = _dot(h, wg_ref[...])
    up = _dot(h, wu_ref[...])
    act = (gate * jax.nn.sigmoid(gate) * up).astype(BF16)
    o_ref[...] += _dot(act, wd_ref[...])

    if final_norm:
        @pl.when(j == pl.num_programs(1) - 1)
        def _():
            o_ref[...] = _rms(o_ref[...], fg_ref[...])


def _ffn_layer(x2, g, wg, wu, wd, fg, *, final_norm, tm=1024, tf=512):
    rows, d = x2.shape
    d_ff = wg.shape[1]
    kern = functools.partial(_ffn_kernel, final_norm=final_norm)
    return pl.pallas_call(
        kern,
        grid=(rows // tm, d_ff // tf),
        in_specs=[
            pl.BlockSpec((tm, d), lambda i, j: (i, 0)),
            _resident((1, d)),
            pl.BlockSpec((d, tf), lambda i, j: (0, j)),
            pl.BlockSpec((d, tf), lambda i, j: (0, j)),
            pl.BlockSpec((tf, d), lambda i, j: (j, 0)),
            _resident((1, d)),
        ],
        out_specs=pl.BlockSpec((tm, d), lambda i, j: (i, 0)),
        out_shape=jax.ShapeDtypeStruct((rows, d), F32),
        scratch_shapes=[pltpu.VMEM((tm, d), BF16)],
        compiler_params=pltpu.CompilerParams(
            dimension_semantics=("arbitrary", "arbitrary"), vmem_limit_bytes=VMEM_LIMIT),
        name="ffn_layer",
    )(x2, g, wg, wu, wd, fg)


def kernel(x, mem, norm_mix_g, w_in, pool_w, pool_scale, sgu_norm_g, w_spatial, b_spatial, w_out,
           norm_xattn_g, norm_mem_g, w_q, w_k, w_v, w_o, norm_ffn_g, w_gate, w_up, w_down, final_norm_g):
    batch, seq, d = x.shape
    n_mem = mem.shape[1]
    depth = w_in.shape[0]
    assert depth >= 1, "the final norm is fused into the last layer"
    d_sgu = sgu_norm_g.shape[1]
    head = d_sgu // N_SGU_HEADS

    x2 = x.reshape(batch * seq, d)
    mem2 = mem.reshape(batch * n_mem, d)
    row = lambda a: a.reshape(1, -1)
    for l in range(depth):
        bias = jnp.repeat(b_spatial[l].T, head, axis=1)
        x2 = _mix_layer(x2, row(norm_mix_g[l]), w_in[l].astype(BF16), pool_w[l].astype(BF16),
                        row(pool_scale[l]), row(sgu_norm_g[l]), w_spatial[l], bias,
                        w_out[l].astype(BF16), seq=seq)
        qk, vo = _kv_proj(mem2, row(norm_mem_g[l]), w_k[l].astype(BF16), w_v[l].astype(BF16),
                          w_q[l].astype(BF16), w_o[l].astype(BF16), batch=batch, n_mem=n_mem)
        x2 = _attn_layer(x2, row(norm_xattn_g[l]), qk, vo, seq=seq)
        x2 = _ffn_layer(x2, row(norm_ffn_g[l]), w_gate[l].astype(BF16), w_up[l].astype(BF16),
                        w_down[l].astype(BF16), row(final_norm_g), final_norm=(l == depth - 1))
    return x2.reshape(batch, seq, d)
```

```python
import functools

import jax
import jax.numpy as jnp
from jax import lax
from jax.experimental import pallas as pl
from jax.experimental.pallas import tpu as pltpu

EPS = 1e-6
CHUNK = 64
POOL_WINDOWS = (2, 4, 8, 16)
SGU_BLOCK = 128
N_SGU_HEADS = 8
N_XATTN_HEADS = 4

LANES = 128
HALO = 16
VMEM_LIMIT = 56 * 1024 * 1024

BF16 = jnp.bfloat16
F32 = jnp.float32


def _rms(x, g):
    ms = jnp.mean(x * x, axis=-1, keepdims=True)
    return x * lax.rsqrt(ms + EPS) * g


def _dot(a, b):
    return jnp.dot(a, b, preferred_element_type=F32)


def _resident(shape):
    return pl.BlockSpec(shape, lambda *_: (0,) * len(shape), pipeline_mode=pl.Buffered(1))


def _cast_specs(weights, n_steps):
    blocks = [pl.BlockSpec((w.shape[0] // n_steps, w.shape[1]), lambda i: (i, 0)) for w in weights]
    shapes = [jax.ShapeDtypeStruct(w.shape, BF16) for w in weights]
    for w in weights:
        assert w.shape[0] % (n_steps * 16) == 0, (w.shape, n_steps)
    return blocks, shapes


def _cast_slabs(src_refs, dst_refs):
    for src, dst in zip(src_refs, dst_refs):
        dst[...] = src[...].astype(BF16)


def _kv_kernel(mem_ref, g_ref, wk_ref, wv_ref, wq_ref, wo_ref, qk_ref, vo_ref, m_ref, *, batch, n_mem):
    @pl.when(pl.program_id(0) == 0)
    def _():
        m_ref[...] = _rms(mem_ref[...], g_ref[...]).astype(BF16)

    m = m_ref[...]
    k = _dot(m, wk_ref[...]).astype(BF16)
    v = _dot(m, wv_ref[...]).astype(BF16)
    scale = wq_ref.shape[1] ** -0.5
    for b in range(batch):
        rows = slice(b * n_mem, (b + 1) * n_mem)
        qk = lax.dot_general(wq_ref[...], k[rows], (((1,), (1,)), ((), ())), preferred_element_type=F32)
        qk_ref[b] = (qk * scale).astype(BF16)
        vo_ref[b] = _dot(v[rows], wo_ref[...]).astype(BF16)


def _kv_proj(mem2, g, wk, wv, wq, wo, *, batch, n_mem):
    rows, d = mem2.shape
    hd = d // N_XATTN_HEADS
    kern = functools.partial(_kv_kernel, batch=batch, n_mem=n_mem)
    return pl.pallas_call(
        kern,
        grid=(N_XATTN_HEADS,),
        in_specs=[
            _resident((rows, d)),
            _resident((1, d)),
            pl.BlockSpec((d, hd), lambda h: (0, h)),
            pl.BlockSpec((d, hd), lambda h: (0, h)),
            pl.BlockSpec((d, hd), lambda h: (0, h)),
            pl.BlockSpec((hd, d), lambda h: (h, 0)),
        ],
        out_specs=[
            pl.BlockSpec((batch, d, n_mem), lambda h: (0, 0, h)),
            pl.BlockSpec((batch, n_mem, d), lambda h: (0, h, 0)),
        ],
        out_shape=[
            jax.ShapeDtypeStruct((batch, d, N_XATTN_HEADS * n_mem), BF16),
            jax.ShapeDtypeStruct((batch, N_XATTN_HEADS * n_mem, d), BF16),
        ],
        scratch_shapes=[pltpu.VMEM((rows, d), BF16)],
        compiler_params=pltpu.CompilerParams(
            dimension_semantics=("arbitrary",), vmem_limit_bytes=VMEM_LIMIT),
        name="kv_proj",
    )(mem2, g, wk, wv, wq, wo)


def _mix_kernel(*refs, n_cast, tm, tiles_per_seq, d_pool, d_sgu):
    (x_ref, g_ref, w_in_ref, pool_w_ref, pool_scale_ref, sgu_g_ref, ws_ref, bias_ref, w_out_ref) = refs[:9]
    cast_src, o_ref, cast_dst = refs[9:9 + n_cast], refs[9 + n_cast], refs[10 + n_cast:10 + 2 * n_cast]
    halo_ref, wsm_ref, y_ref = refs[10 + 2 * n_cast:]
    _cast_slabs(cast_src, cast_dst)
    i = pl.program_id(0)
    seq_tile = i % tiles_per_seq
    group = d_pool // len(POOL_WINDOWS)
    head = d_sgu // N_SGU_HEADS

    @pl.when(i == 0)
    def _():
        t = lax.broadcasted_iota(jnp.int32, (SGU_BLOCK, SGU_BLOCK), 0)
        s = lax.broadcasted_iota(jnp.int32, (SGU_BLOCK, SGU_BLOCK), 1)
        mask = (s // CHUNK) <= (t // CHUNK)
        for h in range(N_SGU_HEADS):
            wsm_ref[h] = jnp.where(mask, ws_ref[h], 0.0).astype(BF16)

    @pl.when(seq_tile == 0)
    def _():
        halo_ref[...] = jnp.zeros_like(halo_ref)

    x = x_ref[...]
    xn = _rms(x, g_ref[...]).astype(BF16)
    a = _dot(xn, w_in_ref[:, 0:d_pool])
    u = _dot(xn, w_in_ref[:, d_pool:d_pool + d_sgu])
    v = _dot(xn, w_in_ref[:, d_pool + d_sgu:d_pool + 2 * d_sgu])

    ext = jnp.concatenate([halo_ref[...], a], axis=0)
    halo_ref[...] = a[tm - HALO:, :]
    pos = (seq_tile * tm + lax.broadcasted_iota(jnp.int32, (tm, 1), 0) + 1).astype(F32)
    for gi, w in enumerate(POOL_WINDOWS):
        cols = slice(gi * group, (gi + 1) * group)
        s = ext[:, cols]
        k = 1
        while k < w:
            s = s + pltpu.roll(s, k, axis=0)
            k *= 2
        inv = 1.0 / jnp.minimum(pos, float(w))
        p = (s[HALO:] * inv - a[:, cols]).astype(BF16)
        y_ref[:, cols] = (_dot(p, pool_w_ref[gi]) * pool_scale_ref[:, cols]).astype(BF16)

    vn = _rms(v, sgu_g_ref[...]).astype(BF16)
    for blk in range(tm // SGU_BLOCK):
        rows = slice(blk * SGU_BLOCK, (blk + 1) * SGU_BLOCK)
        for h in range(N_SGU_HEADS):
            cols = slice(h * head, (h + 1) * head)
            mixed = _dot(wsm_ref[h], vn[rows, cols]) + bias_ref[:, cols]
            y_ref[rows, d_pool + h * head:d_pool + (h + 1) * head] = (u[rows, cols] * mixed).astype(BF16)

    o_ref[...] = x + _dot(y_ref[...], w_out_ref[...])


def _mix_layer(x2, g, w_in, pool_w, pool_scale, sgu_g, ws, bias, w_out, cast_weights, *, seq, tm=512):
    rows, d = x2.shape
    d_in = w_in.shape[1]
    d_mix = w_out.shape[0]
    d_pool = pool_scale.shape[1]
    d_sgu = d_mix - d_pool
    group = d_pool // len(POOL_WINDOWS)
    cast_blocks, cast_shapes = _cast_specs(cast_weights, rows // tm)
    kern = functools.partial(_mix_kernel, n_cast=len(cast_weights), tm=tm, tiles_per_seq=seq // tm,
                             d_pool=d_pool, d_sgu=d_sgu)
    return pl.pallas_call(
        kern,
        grid=(rows // tm,),
        in_specs=[
            pl.BlockSpec((tm, d), lambda i: (i, 0)),
            _resident((1, d)),
            _resident((d, d_in)),
            _resident((len(POOL_WINDOWS), group, group)),
            _resident((1, d_pool)),
            _resident((1, d_sgu)),
            _resident((N_SGU_HEADS, SGU_BLOCK, SGU_BLOCK)),
            _resident((SGU_BLOCK, d_sgu)),
            _resident((d_mix, d)),
            *cast_blocks,
        ],
        out_specs=[pl.BlockSpec((tm, d), lambda i: (i, 0)), *cast_blocks],
        out_shape=[jax.ShapeDtypeStruct((rows, d), F32), *cast_shapes],
        scratch_shapes=[
            pltpu.VMEM((HALO, d_pool), F32),
            pltpu.VMEM((N_SGU_HEADS, SGU_BLOCK, SGU_BLOCK), BF16),
            pltpu.VMEM((tm, d_mix), BF16),
        ],
        compiler_params=pltpu.CompilerParams(
            dimension_semantics=("arbitrary",), vmem_limit_bytes=VMEM_LIMIT),
        name="mix_layer",
    )(x2, g, w_in, pool_w, pool_scale, sgu_g, ws, bias, w_out, *cast_weights)


def _attn_kernel(*refs, n_cast, n_mem):
    x_ref, g_ref, qk_ref, vo_ref = refs[:4]
    cast_src, o_ref, cast_dst = refs[4:4 + n_cast], refs[4 + n_cast], refs[5 + n_cast:]
    _cast_slabs(cast_src, cast_dst)
    x = x_ref[...]
    h = _rms(x, g_ref[...]).astype(BF16)
    s = _dot(h, qk_ref[0])
    ps = []
    for hd in range(N_XATTN_HEADS):
        sh = s[:, hd * n_mem:(hd + 1) * n_mem]
        e = jnp.exp(sh - jnp.max(sh, axis=-1, keepdims=True))
        ps.append((e * (1.0 / jnp.sum(e, axis=-1, keepdims=True))).astype(BF16))
    p = jnp.concatenate(ps, axis=1)
    o_ref[...] = x + _dot(p, vo_ref[0])


def _attn_layer(x2, g, qk, vo, cast_weights, *, seq, tm=512):
    rows, d = x2.shape
    tiles_per_seq = seq // tm
    hm = qk.shape[2]
    cast_blocks, cast_shapes = _cast_specs(cast_weights, rows // tm)
    kern = functools.partial(_attn_kernel, n_cast=len(cast_weights), n_mem=hm // N_XATTN_HEADS)
    return pl.pallas_call(
        kern,
        grid=(rows // tm,),
        in_specs=[
            pl.BlockSpec((tm, d), lambda i: (i, 0)),
            _resident((1, d)),
            pl.BlockSpec((1, d, hm), lambda i: (i // tiles_per_seq, 0, 0)),
            pl.BlockSpec((1, hm, d), lambda i: (i // tiles_per_seq, 0, 0)),
            *cast_blocks,
        ],
        out_specs=[pl.BlockSpec((tm, d), lambda i: (i, 0)), *cast_blocks],
        out_shape=[jax.ShapeDtypeStruct((rows, d), F32), *cast_shapes],
        compiler_params=pltpu.CompilerParams(
            dimension_semantics=("arbitrary",), vmem_limit_bytes=VMEM_LIMIT),
        name="attn_layer",
    )(x2, g, qk, vo, *cast_weights)


def _ffn_kernel(x_ref, g_ref, wg_ref, wu_ref, wd_ref, fg_ref, o_ref, h_ref, *, final_norm):
    j = pl.program_id(1)

    @pl.when(j == 0)
    def _():
        x = x_ref[...]
        h_ref[...] = _rms(x, g_ref[...]).astype(BF16)
        o_ref[...] = x

    h = h_ref[...]
    gate = _dot(h, wg_ref[...])
    up = _dot(h, wu_ref[...])
    act = (gate * jax.nn.sigmoid(gate) * up).astype(BF16)
    o_ref[...] += _dot(act, wd_ref[...])

    if final_norm:
        @pl.when(j == pl.num_programs(1) - 1)
        def _():
            o_ref[...] = _rms(o_ref[...], fg_ref[...])


def _ffn_layer(x2, g, wg, wu, wd, fg, *, final_norm, tm=1024, tf=512):
    rows, d = x2.shape
    d_ff = wg.shape[1]
    kern = functools.partial(_ffn_kernel, final_norm=final_norm)
    return pl.pallas_call(
        kern,
        grid=(rows // tm, d_ff // tf),
        in_specs=[
            pl.BlockSpec((tm, d), lambda i, j: (i, 0)),
            _resident((1, d)),
            pl.BlockSpec((d, tf), lambda i, j: (0, j)),
            pl.BlockSpec((d, tf), lambda i, j: (0, j)),
            pl.BlockSpec((tf, d), lambda i, j: (j, 0)),
            _resident((1, d)),
        ],
        out_specs=pl.BlockSpec((tm, d), lambda i, j: (i, 0)),
        out_shape=jax.ShapeDtypeStruct((rows, d), F32),
        scratch_shapes=[pltpu.VMEM((tm, d), BF16)],
        compiler_params=pltpu.CompilerParams(
            dimension_semantics=("arbitrary", "arbitrary"), vmem_limit_bytes=VMEM_LIMIT),
        name="ffn_layer",
    )(x2, g, wg, wu, wd, fg)


def kernel(x, mem, norm_mix_g, w_in, pool_w, pool_scale, sgu_norm_g, w_spatial, b_spatial, w_out,
           norm_xattn_g, norm_mem_g, w_q, w_k, w_v, w_o, norm_ffn_g, w_gate, w_up, w_down, final_norm_g):
    batch, seq, d = x.shape
    n_mem = mem.shape[1]
    depth = w_in.shape[0]
    assert depth >= 1, "the final norm is fused into the last layer"
    d_sgu = sgu_norm_g.shape[1]
    head = d_sgu // N_SGU_HEADS

    x2 = x.reshape(batch * seq, d)
    mem2 = mem.reshape(batch * n_mem, d)
    row = lambda a: a.reshape(1, -1)
    for l in range(depth):
        bias = jnp.repeat(b_spatial[l].T, head, axis=1)
        x2, wk, wv, wq, wo = _mix_layer(
            x2, row(norm_mix_g[l]), w_in[l].astype(BF16), pool_w[l].astype(BF16), row(pool_scale[l]),
            row(sgu_norm_g[l]), w_spatial[l], bias, w_out[l].astype(BF16),
            (w_k[l], w_v[l], w_q[l], w_o[l]), seq=seq)
        qk, vo = _kv_proj(mem2, row(norm_mem_g[l]), wk, wv, wq, wo, batch=batch, n_mem=n_mem)
        x2, wg, wu, wd = _attn_layer(x2, row(norm_xattn_g[l]), qk, vo, (w_gate[l], w_up[l], w_down[l]), seq=seq)
        x2 = _ffn_layer(x2, row(norm_ffn_g[l]), wg, wu, wd, row(final_norm_g), final_norm=(l == depth - 1))
    return x2.reshape(batch, seq, d)
```

```python
import functools

import jax
import jax.numpy as jnp
from jax import lax
from jax.experimental import pallas as pl
from jax.experimental.pallas import tpu as pltpu

EPS = 1e-6
CHUNK = 64
POOL_WINDOWS = (2, 4, 8, 16)
SGU_BLOCK = 128
N_SGU_HEADS = 8
N_XATTN_HEADS = 4

LANES = 128
HALO = 16
VMEM_LIMIT = 56 * 1024 * 1024

BF16 = jnp.bfloat16
F32 = jnp.float32


def _rms(x, g):
    ms = jnp.mean(x * x, axis=-1, keepdims=True)
    return x * lax.rsqrt(ms + EPS) * g


def _dot(a, b):
    return jnp.dot(a, b, preferred_element_type=F32)


def _resident(shape):
    return pl.BlockSpec(shape, lambda *_: (0,) * len(shape), pipeline_mode=pl.Buffered(1))


def _cast_specs(weights, n_steps):
    blocks = [pl.BlockSpec((w.shape[0] // n_steps, w.shape[1]), lambda i: (i, 0)) for w in weights]
    shapes = [jax.ShapeDtypeStruct(w.shape, BF16) for w in weights]
    for w in weights:
        assert w.shape[0] % (n_steps * 16) == 0, (w.shape, n_steps)
    return blocks, shapes


def _cast_slabs(src_refs, dst_refs):
    for src, dst in zip(src_refs, dst_refs):
        dst[...] = src[...].astype(BF16)


def _kv_kernel(mem_ref, g_ref, wk_ref, wv_ref, wq_ref, wo_ref, qk_ref, vo_ref, m_ref, *, batch, n_mem):
    @pl.when(pl.program_id(0) == 0)
    def _():
        m_ref[...] = _rms(mem_ref[...], g_ref[...]).astype(BF16)

    m = m_ref[...]
    k = _dot(m, wk_ref[...]).astype(BF16)
    v = _dot(m, wv_ref[...]).astype(BF16)
    scale = wq_ref.shape[1] ** -0.5
    for b in range(batch):
        rows = slice(b * n_mem, (b + 1) * n_mem)
        qk = lax.dot_general(wq_ref[...], k[rows], (((1,), (1,)), ((), ())), preferred_element_type=F32)
        qk_ref[b] = (qk * scale).astype(BF16)
        vo_ref[b] = _dot(v[rows], wo_ref[...]).astype(BF16)


def _kv_proj(mem2, g, wk, wv, wq, wo, *, batch, n_mem):
    rows, d = mem2.shape
    hd = d // N_XATTN_HEADS
    kern = functools.partial(_kv_kernel, batch=batch, n_mem=n_mem)
    return pl.pallas_call(
        kern,
        grid=(N_XATTN_HEADS,),
        in_specs=[
            _resident((rows, d)),
            _resident((1, d)),
            pl.BlockSpec((d, hd), lambda h: (0, h)),
            pl.BlockSpec((d, hd), lambda h: (0, h)),
            pl.BlockSpec((d, hd), lambda h: (0, h)),
            pl.BlockSpec((hd, d), lambda h: (h, 0)),
        ],
        out_specs=[
            pl.BlockSpec((batch, d, n_mem), lambda h: (0, 0, h)),
            pl.BlockSpec((batch, n_mem, d), lambda h: (0, h, 0)),
        ],
        out_shape=[
            jax.ShapeDtypeStruct((batch, d, N_XATTN_HEADS * n_mem), BF16),
            jax.ShapeDtypeStruct((batch, N_XATTN_HEADS * n_mem, d), BF16),
        ],
        scratch_shapes=[pltpu.VMEM((rows, d), BF16)],
        compiler_params=pltpu.CompilerParams(
            dimension_semantics=("arbitrary",), vmem_limit_bytes=VMEM_LIMIT),
        name="kv_proj",
    )(mem2, g, wk, wv, wq, wo)


def _prep_kernel(pool_w_ref, pool_scale_ref, ws_ref, w_out_ref, w_eff_ref, wsm_ref, *, d_pool):
    group = d_pool // len(POOL_WINDOWS)
    for gi in range(len(POOL_WINDOWS)):
        rows = slice(gi * group, (gi + 1) * group)
        pw = (pool_w_ref[gi] * pool_scale_ref[:, rows]).astype(BF16)
        w_eff_ref[rows, :] = _dot(pw, w_out_ref[rows, :].astype(BF16)).astype(BF16)
    w_eff_ref[d_pool:, :] = w_out_ref[d_pool:, :].astype(BF16)

    t = lax.broadcasted_iota(jnp.int32, (SGU_BLOCK, SGU_BLOCK), 0)
    s = lax.broadcasted_iota(jnp.int32, (SGU_BLOCK, SGU_BLOCK), 1)
    mask = (s // CHUNK) <= (t // CHUNK)
    for h in range(N_SGU_HEADS):
        wsm_ref[h] = jnp.where(mask, ws_ref[h], 0.0).astype(BF16)


def _prep_weights(pool_w, pool_scale, ws, w_out, *, tn=512):
    d_mix, d = w_out.shape
    d_pool = pool_scale.shape[1]
    kern = functools.partial(_prep_kernel, d_pool=d_pool)
    return pl.pallas_call(
        kern,
        grid=(d // tn,),
        in_specs=[
            _resident(pool_w.shape),
            _resident((1, d_pool)),
            _resident(ws.shape),
            pl.BlockSpec((d_mix, tn), lambda j: (0, j)),
        ],
        out_specs=[
            pl.BlockSpec((d_mix, tn), lambda j: (0, j)),
            pl.BlockSpec(ws.shape, lambda j: (0, 0, 0)),
        ],
        out_shape=[jax.ShapeDtypeStruct((d_mix, d), BF16), jax.ShapeDtypeStruct(ws.shape, BF16)],
        compiler_params=pltpu.CompilerParams(
            dimension_semantics=("arbitrary",), vmem_limit_bytes=VMEM_LIMIT),
        name="prep_weights",
    )(pool_w, pool_scale, ws, w_out)


def _mix_kernel(*refs, n_cast, tm, tiles_per_seq, d_pool, d_sgu):
    x_ref, g_ref, w_in_ref, sgu_g_ref, wsm_ref, bias_ref, w_out_ref = refs[:7]
    cast_src, o_ref, cast_dst = refs[7:7 + n_cast], refs[7 + n_cast], refs[8 + n_cast:8 + 2 * n_cast]
    halo_ref, y_ref = refs[8 + 2 * n_cast:]
    _cast_slabs(cast_src, cast_dst)
    seq_tile = pl.program_id(0) % tiles_per_seq
    group = d_pool // len(POOL_WINDOWS)
    head = d_sgu // N_SGU_HEADS
    n_blk = tm // SGU_BLOCK

    @pl.when(seq_tile == 0)
    def _():
        halo_ref[...] = jnp.zeros_like(halo_ref)

    x = x_ref[...]
    xn = _rms(x, g_ref[...]).astype(BF16)
    a = _dot(xn, w_in_ref[:, 0:d_pool])
    v = _dot(xn, w_in_ref[:, d_pool + d_sgu:d_pool + 2 * d_sgu])
    u = _dot(xn, w_in_ref[:, d_pool:d_pool + d_sgu])

    ext = jnp.concatenate([halo_ref[...], a], axis=0)
    halo_ref[...] = a[tm - HALO:, :]
    pos = (seq_tile * tm + lax.broadcasted_iota(jnp.int32, (tm, 1), 0) + 1).astype(F32)
    for gi, w in enumerate(POOL_WINDOWS):
        cols = slice(gi * group, (gi + 1) * group)
        s = ext[:, cols]
        k = 1
        while k < w:
            s = s + pltpu.roll(s, k, axis=0)
            k *= 2
        inv = 1.0 / jnp.minimum(pos, float(w))
        y_ref[:, cols] = (s[HALO:] * inv - a[:, cols]).astype(BF16)

    vn = _rms(v, sgu_g_ref[...]).astype(BF16)
    for h in range(N_SGU_HEADS):
        cols = slice(h * head, (h + 1) * head)
        v_h = jnp.concatenate([vn[b * SGU_BLOCK:(b + 1) * SGU_BLOCK, cols] for b in range(n_blk)], axis=1)
        mixed = _dot(wsm_ref[h], v_h)
        for b in range(n_blk):
            rows = slice(b * SGU_BLOCK, (b + 1) * SGU_BLOCK)
            m_b = mixed[:, b * head:(b + 1) * head] + bias_ref[:, cols]
            y_ref[rows, d_pool + h * head:d_pool + (h + 1) * head] = (u[rows, cols] * m_b).astype(BF16)

    o_ref[...] = x + _dot(y_ref[...], w_out_ref[...])


def _mix_layer(x2, g, w_in, sgu_g, wsm, bias, w_out_eff, cast_weights, *, seq, tm=512):
    rows, d = x2.shape
    d_in = w_in.shape[1]
    d_mix = w_out_eff.shape[0]
    d_sgu = sgu_g.shape[1]
    d_pool = d_mix - d_sgu
    cast_blocks, cast_shapes = _cast_specs(cast_weights, rows // tm)
    kern = functools.partial(_mix_kernel, n_cast=len(cast_weights), tm=tm, tiles_per_seq=seq // tm,
                             d_pool=d_pool, d_sgu=d_sgu)
    return pl.pallas_call(
        kern,
        grid=(rows // tm,),
        in_specs=[
            pl.BlockSpec((tm, d), lambda i: (i, 0)),
            _resident((1, d)),
            _resident((d, d_in)),
            _resident((1, d_sgu)),
            _resident((N_SGU_HEADS, SGU_BLOCK, SGU_BLOCK)),
            _resident((SGU_BLOCK, d_sgu)),
            _resident((d_mix, d)),
            *cast_blocks,
        ],
        out_specs=[pl.BlockSpec((tm, d), lambda i: (i, 0)), *cast_blocks],
        out_shape=[jax.ShapeDtypeStruct((rows, d), F32), *cast_shapes],
        scratch_shapes=[
            pltpu.VMEM((HALO, d_pool), F32),
            pltpu.VMEM((tm, d_mix), BF16),
        ],
        compiler_params=pltpu.CompilerParams(
            dimension_semantics=("arbitrary",), vmem_limit_bytes=VMEM_LIMIT),
        name="mix_layer",
    )(x2, g, w_in, sgu_g, wsm, bias, w_out_eff, *cast_weights)


def _attn_kernel(*refs, n_cast, n_mem):
    x_ref, g_ref, qk_ref, vo_ref = refs[:4]
    cast_src, o_ref, cast_dst = refs[4:4 + n_cast], refs[4 + n_cast], refs[5 + n_cast:]
    _cast_slabs(cast_src, cast_dst)
    x = x_ref[...]
    h = _rms(x, g_ref[...]).astype(BF16)
    s = _dot(h, qk_ref[0])
    ps = []
    for hd in range(N_XATTN_HEADS):
        sh = s[:, hd * n_mem:(hd + 1) * n_mem]
        e = jnp.exp(sh - jnp.max(sh, axis=-1, keepdims=True))
        ps.append((e * (1.0 / jnp.sum(e, axis=-1, keepdims=True))).astype(BF16))
    p = jnp.concatenate(ps, axis=1)
    o_ref[...] = x + _dot(p, vo_ref[0])


def _attn_layer(x2, g, qk, vo, cast_weights, *, seq, tm=512):
    rows, d = x2.shape
    tiles_per_seq = seq // tm
    hm = qk.shape[2]
    cast_blocks, cast_shapes = _cast_specs(cast_weights, rows // tm)
    kern = functools.partial(_attn_kernel, n_cast=len(cast_weights), n_mem=hm // N_XATTN_HEADS)
    return pl.pallas_call(
        kern,
        grid=(rows // tm,),
        in_specs=[
            pl.BlockSpec((tm, d), lambda i: (i, 0)),
            _resident((1, d)),
            pl.BlockSpec((1, d, hm), lambda i: (i // tiles_per_seq, 0, 0)),
            pl.BlockSpec((1, hm, d), lambda i: (i // tiles_per_seq, 0, 0)),
            *cast_blocks,
        ],
        out_specs=[pl.BlockSpec((tm, d), lambda i: (i, 0)), *cast_blocks],
        out_shape=[jax.ShapeDtypeStruct((rows, d), F32), *cast_shapes],
        compiler_params=pltpu.CompilerParams(
            dimension_semantics=("arbitrary",), vmem_limit_bytes=VMEM_LIMIT),
        name="attn_layer",
    )(x2, g, qk, vo, *cast_weights)


def _ffn_kernel(x_ref, g_ref, wg_ref, wu_ref, wd_ref, fg_ref, o_ref, h_ref, *, final_norm):
    j = pl.program_id(1)

    @pl.when(j == 0)
    def _():
        x = x_ref[...]
        h_ref[...] = _rms(x, g_ref[...]).astype(BF16)
        o_ref[...] = x

    h = h_ref[...]
    gate = _dot(h, wg_ref[...])
    up = _dot(h, wu_ref[...])
    act = (gate * jax.nn.sigmoid(gate) * up).astype(BF16)
    o_ref[...] += _dot(act, wd_ref[...])

    if final_norm:
        @pl.when(j == pl.num_programs(1) - 1)
        def _():
            o_ref[...] = _rms(o_ref[...], fg_ref[...])


def _ffn_layer(x2, g, wg, wu, wd, fg, *, final_norm, tm=1024, tf=512):
    rows, d = x2.shape
    d_ff = wg.shape[1]
    kern = functools.partial(_ffn_kernel, final_norm=final_norm)
    return pl.pallas_call(
        kern,
        grid=(rows // tm, d_ff // tf),
        in_specs=[
            pl.BlockSpec((tm, d), lambda i, j: (i, 0)),
            _resident((1, d)),
            pl.BlockSpec((d, tf), lambda i, j: (0, j)),
            pl.BlockSpec((d, tf), lambda i, j: (0, j)),
            pl.BlockSpec((tf, d), lambda i, j: (j, 0)),
            _resident((1, d)),
        ],
        out_specs=pl.BlockSpec((tm, d), lambda i, j: (i, 0)),
        out_shape=jax.ShapeDtypeStruct((rows, d), F32),
        scratch_shapes=[pltpu.VMEM((tm, d), BF16)],
        compiler_params=pltpu.CompilerParams(
            dimension_semantics=("arbitrary", "arbitrary"), vmem_limit_bytes=VMEM_LIMIT),
        name="ffn_layer",
    )(x2, g, wg, wu, wd, fg)


def kernel(x, mem, norm_mix_g, w_in, pool_w, pool_scale, sgu_norm_g, w_spatial, b_spatial, w_out,
           norm_xattn_g, norm_mem_g, w_q, w_k, w_v, w_o, norm_ffn_g, w_gate, w_up, w_down, final_norm_g):
    batch, seq, d = x.shape
    n_mem = mem.shape[1]
    depth = w_in.shape[0]
    assert depth >= 1, "the final norm is fused into the last layer"
    d_sgu = sgu_norm_g.shape[1]
    head = d_sgu // N_SGU_HEADS

    x2 = x.reshape(batch * seq, d)
    mem2 = mem.reshape(batch * n_mem, d)
    row = lambda a: a.reshape(1, -1)
    for l in range(depth):
        bias = jnp.repeat(b_spatial[l].T, head, axis=1)
        w_out_eff, wsm = _prep_weights(pool_w[l], row(pool_scale[l]), w_spatial[l], w_out[l])
        x2, wk, wv, wq, wo, wg = _mix_layer(
            x2, row(norm_mix_g[l]), w_in[l].astype(BF16), row(sgu_norm_g[l]), wsm, bias, w_out_eff,
            (w_k[l], w_v[l], w_q[l], w_o[l], w_gate[l]), seq=seq)
        qk, vo = _kv_proj(mem2, row(norm_mem_g[l]), wk, wv, wq, wo, batch=batch, n_mem=n_mem)
        x2, wu, wd = _attn_layer(x2, row(norm_xattn_g[l]), qk, vo, (w_up[l], w_down[l]), seq=seq)
        x2 = _ffn_layer(x2, row(norm_ffn_g[l]), wg, wu, wd, row(final_norm_g), final_norm=(l == depth - 1))
    return x2.reshape(batch, seq, d)
```

```python
import functools

import jax
import jax.numpy as jnp
from jax import lax
from jax.experimental import pallas as pl
from jax.experimental.pallas import tpu as pltpu

EPS = 1e-6
CHUNK = 64
POOL_WINDOWS = (2, 4, 8, 16)
SGU_BLOCK = 128
N_SGU_HEADS = 8
N_XATTN_HEADS = 4

LANES = 128
HALO = 16
VMEM_LIMIT = 56 * 1024 * 1024

BF16 = jnp.bfloat16
F32 = jnp.float32


def _rms(x, g):
    ms = jnp.mean(x * x, axis=-1, keepdims=True)
    return x * lax.rsqrt(ms + EPS) * g


def _dot(a, b):
    return jnp.dot(a, b, preferred_element_type=F32)


def _resident(shape):
    return pl.BlockSpec(shape, lambda *_: (0,) * len(shape), pipeline_mode=pl.Buffered(1))


def _cast_specs(weights, n_steps):
    blocks = [pl.BlockSpec((w.shape[0] // n_steps, w.shape[1]), lambda i: (i, 0)) for w in weights]
    shapes = [jax.ShapeDtypeStruct(w.shape, BF16) for w in weights]
    for w in weights:
        assert w.shape[0] % (n_steps * 16) == 0, (w.shape, n_steps)
    return blocks, shapes


def _cast_slabs(src_refs, dst_refs):
    for src, dst in zip(src_refs, dst_refs):
        dst[...] = src[...].astype(BF16)


def _kv_kernel(mem_ref, g_ref, wk_ref, wv_ref, wq_ref, wo_ref, qk_ref, vo_ref, m_ref, *, batch, n_mem):
    @pl.when(pl.program_id(0) == 0)
    def _():
        m_ref[...] = _rms(mem_ref[...], g_ref[...]).astype(BF16)

    m = m_ref[...]
    k = _dot(m, wk_ref[...]).astype(BF16)
    v = _dot(m, wv_ref[...]).astype(BF16)
    scale = wq_ref.shape[1] ** -0.5
    for b in range(batch):
        rows = slice(b * n_mem, (b + 1) * n_mem)
        qk = lax.dot_general(wq_ref[...], k[rows], (((1,), (1,)), ((), ())), preferred_element_type=F32)
        qk_ref[b] = (qk * scale).astype(BF16)
        vo_ref[b] = _dot(v[rows], wo_ref[...]).astype(BF16)


def _kv_proj(mem2, g, wk, wv, wq, wo, *, batch, n_mem):
    rows, d = mem2.shape
    hd = d // N_XATTN_HEADS
    kern = functools.partial(_kv_kernel, batch=batch, n_mem=n_mem)
    return pl.pallas_call(
        kern,
        grid=(N_XATTN_HEADS,),
        in_specs=[
            _resident((rows, d)),
            _resident((1, d)),
            pl.BlockSpec((d, hd), lambda h: (0, h)),
            pl.BlockSpec((d, hd), lambda h: (0, h)),
            pl.BlockSpec((d, hd), lambda h: (0, h)),
            pl.BlockSpec((hd, d), lambda h: (h, 0)),
        ],
        out_specs=[
            pl.BlockSpec((batch, d, n_mem), lambda h: (0, 0, h)),
            pl.BlockSpec((batch, n_mem, d), lambda h: (0, h, 0)),
        ],
        out_shape=[
            jax.ShapeDtypeStruct((batch, d, N_XATTN_HEADS * n_mem), BF16),
            jax.ShapeDtypeStruct((batch, N_XATTN_HEADS * n_mem, d), BF16),
        ],
        scratch_shapes=[pltpu.VMEM((rows, d), BF16)],
        compiler_params=pltpu.CompilerParams(
            dimension_semantics=("arbitrary",), vmem_limit_bytes=VMEM_LIMIT),
        name="kv_proj",
    )(mem2, g, wk, wv, wq, wo)


def _prep_kernel(pool_w_ref, pool_scale_ref, ws_ref, w_out_ref, w_eff_ref, wsm_ref, *, d_pool):
    group = d_pool // len(POOL_WINDOWS)
    for gi in range(len(POOL_WINDOWS)):
        rows = slice(gi * group, (gi + 1) * group)
        pw = (pool_w_ref[gi] * pool_scale_ref[:, rows]).astype(BF16)
        w_eff_ref[rows, :] = _dot(pw, w_out_ref[rows, :].astype(BF16)).astype(BF16)
    w_eff_ref[d_pool:, :] = w_out_ref[d_pool:, :].astype(BF16)

    t = lax.broadcasted_iota(jnp.int32, (SGU_BLOCK, SGU_BLOCK), 0)
    s = lax.broadcasted_iota(jnp.int32, (SGU_BLOCK, SGU_BLOCK), 1)
    mask = (s // CHUNK) <= (t // CHUNK)
    for h in range(N_SGU_HEADS):
        wsm_ref[h] = jnp.where(mask, ws_ref[h], 0.0).astype(BF16)


def _prep_weights(pool_w, pool_scale, ws, w_out, *, tn=512):
    d_mix, d = w_out.shape
    d_pool = pool_scale.shape[1]
    kern = functools.partial(_prep_kernel, d_pool=d_pool)
    return pl.pallas_call(
        kern,
        grid=(d // tn,),
        in_specs=[
            _resident(pool_w.shape),
            _resident((1, d_pool)),
            _resident(ws.shape),
            pl.BlockSpec((d_mix, tn), lambda j: (0, j)),
        ],
        out_specs=[
            pl.BlockSpec((d_mix, tn), lambda j: (0, j)),
            pl.BlockSpec(ws.shape, lambda j: (0, 0, 0)),
        ],
        out_shape=[jax.ShapeDtypeStruct((d_mix, d), BF16), jax.ShapeDtypeStruct(ws.shape, BF16)],
        compiler_params=pltpu.CompilerParams(
            dimension_semantics=("arbitrary",), vmem_limit_bytes=VMEM_LIMIT),
        name="prep_weights",
    )(pool_w, pool_scale, ws, w_out)


def _mix_kernel(*refs, n_cast, tm, tiles_per_seq, d_pool, d_sgu):
    x_ref, g_ref, w_in_ref, sgu_g_ref, wsm_ref, bias_ref, w_out_ref = refs[:7]
    cast_src, o_ref, cast_dst = refs[7:7 + n_cast], refs[7 + n_cast], refs[8 + n_cast:8 + 2 * n_cast]
    halo_ref, y_ref = refs[8 + 2 * n_cast:]
    _cast_slabs(cast_src, cast_dst)
    seq_tile = pl.program_id(0) % tiles_per_seq
    group = d_pool // len(POOL_WINDOWS)
    head = d_sgu // N_SGU_HEADS
    n_blk = tm // SGU_BLOCK

    @pl.when(seq_tile == 0)
    def _():
        halo_ref[...] = jnp.zeros_like(halo_ref)

    x = x_ref[...]
    xn = _rms(x, g_ref[...]).astype(BF16)
    a = _dot(xn, w_in_ref[:, 0:d_pool])
    v = _dot(xn, w_in_ref[:, d_pool + d_sgu:d_pool + 2 * d_sgu])
    u = _dot(xn, w_in_ref[:, d_pool:d_pool + d_sgu])

    ext = jnp.concatenate([halo_ref[...], a], axis=0)
    halo_ref[...] = a[tm - HALO:, :]
    pos = (seq_tile * tm + lax.broadcasted_iota(jnp.int32, (tm, 1), 0) + 1).astype(F32)
    for gi, w in enumerate(POOL_WINDOWS):
        cols = slice(gi * group, (gi + 1) * group)
        s = ext[:, cols]
        k = 1
        while k < w:
            s = s + pltpu.roll(s, k, axis=0)
            k *= 2
        inv = 1.0 / jnp.minimum(pos, float(w))
        y_ref[:, cols] = (s[HALO:] * inv - a[:, cols]).astype(BF16)

    vn = _rms(v, sgu_g_ref[...]).astype(BF16)
    for h in range(N_SGU_HEADS):
        cols = slice(h * head, (h + 1) * head)
        v_h = jnp.concatenate([vn[b * SGU_BLOCK:(b + 1) * SGU_BLOCK, cols] for b in range(n_blk)], axis=1)
        mixed = _dot(wsm_ref[h], v_h)
        for b in range(n_blk):
            rows = slice(b * SGU_BLOCK, (b + 1) * SGU_BLOCK)
            m_b = mixed[:, b * head:(b + 1) * head] + bias_ref[:, cols]
            y_ref[rows, d_pool + h * head:d_pool + (h + 1) * head] = (u[rows, cols] * m_b).astype(BF16)

    o_ref[...] = x + _dot(y_ref[...], w_out_ref[...])


def _mix_layer(x2, g, w_in, sgu_g, wsm, bias, w_out_eff, cast_weights, *, seq, tm=512):
    rows, d = x2.shape
    d_in = w_in.shape[1]
    d_mix = w_out_eff.shape[0]
    d_sgu = sgu_g.shape[1]
    d_pool = d_mix - d_sgu
    cast_blocks, cast_shapes = _cast_specs(cast_weights, rows // tm)
    kern = functools.partial(_mix_kernel, n_cast=len(cast_weights), tm=tm, tiles_per_seq=seq // tm,
                             d_pool=d_pool, d_sgu=d_sgu)
    return pl.pallas_call(
        kern,
        grid=(rows // tm,),
        in_specs=[
            pl.BlockSpec((tm, d), lambda i: (i, 0)),
            _resident((1, d)),
            _resident((d, d_in)),
            _resident((1, d_sgu)),
            _resident((N_SGU_HEADS, SGU_BLOCK, SGU_BLOCK)),
            _resident((SGU_BLOCK, d_sgu)),
            _resident((d_mix, d)),
            *cast_blocks,
        ],
        out_specs=[pl.BlockSpec((tm, d), lambda i: (i, 0)), *cast_blocks],
        out_shape=[jax.ShapeDtypeStruct((rows, d), F32), *cast_shapes],
        scratch_shapes=[
            pltpu.VMEM((HALO, d_pool), F32),
            pltpu.VMEM((tm, d_mix), BF16),
        ],
        compiler_params=pltpu.CompilerParams(
            dimension_semantics=("arbitrary",), vmem_limit_bytes=VMEM_LIMIT),
        name="mix_layer",
    )(x2, g, w_in, sgu_g, wsm, bias, w_out_eff, *cast_weights)


def _attn_kernel(*refs, n_cast, n_mem):
    x_ref, g_ref, qk_ref, vo_ref = refs[:4]
    cast_src, o_ref, cast_dst = refs[4:4 + n_cast], refs[4 + n_cast], refs[5 + n_cast:]
    _cast_slabs(cast_src, cast_dst)
    x = x_ref[...]
    h = _rms(x, g_ref[...]).astype(BF16)
    s = _dot(h, qk_ref[0])
    ps = []
    for hd in range(N_XATTN_HEADS):
        sh = s[:, hd * n_mem:(hd + 1) * n_mem]
        e = jnp.exp(sh - jnp.max(sh, axis=-1, keepdims=True))
        ps.append((e * (1.0 / jnp.sum(e, axis=-1, keepdims=True))).astype(BF16))
    p = jnp.concatenate(ps, axis=1)
    o_ref[...] = x + _dot(p, vo_ref[0])


def _attn_layer(x2, g, qk, vo, cast_weights, *, seq, tm=512):
    rows, d = x2.shape
    tiles_per_seq = seq // tm
    hm = qk.shape[2]
    cast_blocks, cast_shapes = _cast_specs(cast_weights, rows // tm)
    kern = functools.partial(_attn_kernel, n_cast=len(cast_weights), n_mem=hm // N_XATTN_HEADS)
    return pl.pallas_call(
        kern,
        grid=(rows // tm,),
        in_specs=[
            pl.BlockSpec((tm, d), lambda i: (i, 0)),
            _resident((1, d)),
            pl.BlockSpec((1, d, hm), lambda i: (i // tiles_per_seq, 0, 0)),
            pl.BlockSpec((1, hm, d), lambda i: (i // tiles_per_seq, 0, 0)),
            *cast_blocks,
        ],
        out_specs=[pl.BlockSpec((tm, d), lambda i: (i, 0)), *cast_blocks],
        out_shape=[jax.ShapeDtypeStruct((rows, d), F32), *cast_shapes],
        compiler_params=pltpu.CompilerParams(
            dimension_semantics=("arbitrary",), vmem_limit_bytes=VMEM_LIMIT),
        name="attn_layer",
    )(x2, g, qk, vo, *cast_weights)


def _ffn_kernel(x_ref, g_ref, wg_ref, wu_ref, wd_ref, fg_ref, o_ref, h_ref, *, final_norm):
    j = pl.program_id(1)

    @pl.when(j == 0)
    def _():
        x = x_ref[...]
        h_ref[...] = _rms(x, g_ref[...]).astype(BF16)
        o_ref[...] = x

    h = h_ref[...]
    gate = _dot(h, wg_ref[...])
    up = _dot(h, wu_ref[...])
    act = (gate * jax.nn.sigmoid(gate) * up).astype(BF16)
    o_ref[...] += _dot(act, wd_ref[...])

    if final_norm:
        @pl.when(j == pl.num_programs(1) - 1)
        def _():
            o_ref[...] = _rms(o_ref[...], fg_ref[...])


def _ffn_layer(x2, g, wg, wu, wd, fg, *, final_norm, tm=1024, tf=512):
    rows, d = x2.shape
    d_ff = wg.shape[1]
    kern = functools.partial(_ffn_kernel, final_norm=final_norm)
    return pl.pallas_call(
        kern,
        grid=(rows // tm, d_ff // tf),
        in_specs=[
            pl.BlockSpec((tm, d), lambda i, j: (i, 0)),
            _resident((1, d)),
            pl.BlockSpec((d, tf), lambda i, j: (0, j)),
            pl.BlockSpec((d, tf), lambda i, j: (0, j)),
            pl.BlockSpec((tf, d), lambda i, j: (j, 0)),
            _resident((1, d)),
        ],
        out_specs=pl.BlockSpec((tm, d), lambda i, j: (i, 0)),
        out_shape=jax.ShapeDtypeStruct((rows, d), F32),
        scratch_shapes=[pltpu.VMEM((tm, d), BF16)],
        compiler_params=pltpu.CompilerParams(
            dimension_semantics=("arbitrary", "arbitrary"), vmem_limit_bytes=VMEM_LIMIT),
        name="ffn_layer",
    )(x2, g, wg, wu, wd, fg)


def kernel(x, mem, norm_mix_g, w_in, pool_w, pool_scale, sgu_norm_g, w_spatial, b_spatial, w_out,
           norm_xattn_g, norm_mem_g, w_q, w_k, w_v, w_o, norm_ffn_g, w_gate, w_up, w_down, final_norm_g):
    batch, seq, d = x.shape
    n_mem = mem.shape[1]
    depth = w_in.shape[0]
    assert depth >= 1, "the final norm is fused into the last layer"
    d_sgu = sgu_norm_g.shape[1]
    head = d_sgu // N_SGU_HEADS

    x2 = x.reshape(batch * seq, d)
    mem2 = mem.reshape(batch * n_mem, d)
    row = lambda a: a.reshape(1, -1)
    for l in range(depth):
        bias = jnp.repeat(b_spatial[l].T, head, axis=1)
        w_out_eff, wsm = _prep_weights(pool_w[l], row(pool_scale[l]), w_spatial[l], w_out[l])
        x2, wk, wv, wq, wo = _mix_layer(
            x2, row(norm_mix_g[l]), w_in[l].astype(BF16), row(sgu_norm_g[l]), wsm, bias, w_out_eff,
            (w_k[l], w_v[l], w_q[l], w_o[l]), seq=seq)
        qk, vo = _kv_proj(mem2, row(norm_mem_g[l]), wk, wv, wq, wo, batch=batch, n_mem=n_mem)
        x2, wg, wu, wd = _attn_layer(x2, row(norm_xattn_g[l]), qk, vo, (w_gate[l], w_up[l], w_down[l]), seq=seq)
        x2 = _ffn_layer(x2, row(norm_ffn_g[l]), wg, wu, wd, row(final_norm_g), final_norm=(l == depth - 1))
    return x2.reshape(batch, seq, d)
```

```python
import functools

import jax
import jax.numpy as jnp
from jax import lax
from jax.experimental import pallas as pl
from jax.experimental.pallas import tpu as pltpu

EPS = 1e-6
CHUNK = 64
POOL_WINDOWS = (2, 4, 8, 16)
SGU_BLOCK = 128
N_SGU_HEADS = 8
N_XATTN_HEADS = 4

HALO = 16
BF16_SUBLANES = 16
VMEM_LIMIT = 62 * 1024 * 1024

BF16 = jnp.bfloat16
F32 = jnp.float32


def _normalize(x):
    ms = jnp.mean(x * x, axis=-1, keepdims=True)
    return x * lax.rsqrt(ms + EPS)


def _rms(x, g):
    return _normalize(x) * g


def _dot(a, b):
    return jnp.dot(a, b, preferred_element_type=F32)


def _resident(shape):
    return pl.BlockSpec(shape, lambda *_: (0,) * len(shape), pipeline_mode=pl.Buffered(1))


def _cast_specs(jobs, n_steps):
    operands, in_specs, out_specs, out_shapes = [], [], [], []
    for w, scale in jobs:
        slab = w.shape[0] // n_steps
        assert slab * n_steps == w.shape[0] and slab % BF16_SUBLANES == 0, (w.shape, n_steps)
        operands.append(w)
        in_specs.append(pl.BlockSpec((slab, w.shape[1]), lambda i: (i, 0)))
        if scale is not None:
            operands.append(scale)
            in_specs.append(pl.BlockSpec((slab, 1), lambda i: (i, 0)))
        out_specs.append(pl.BlockSpec((slab, w.shape[1]), lambda i: (i, 0)))
        out_shapes.append(jax.ShapeDtypeStruct(w.shape, BF16))
    return operands, in_specs, out_specs, out_shapes


def _cast_slabs(src_refs, dst_refs, scaled):
    src = iter(src_refs)
    for dst, has_scale in zip(dst_refs, scaled):
        w = next(src)[...]
        if has_scale:
            w = w * next(src)[...]
        dst[...] = w.astype(BF16)


def _mix_kernel(*refs, scaled, tm, tiles_per_seq, d_pool, d_sgu):
    (x_ref, w_in_ref, pool_w_ref, pool_scale_ref, sgu_g_ref, ws_ref, bias_ref, w_out_ref) = refs[:8]
    n_src, n_dst = len(scaled) + sum(scaled), len(scaled)
    cast_src, o_ref, cast_dst = refs[8:8 + n_src], refs[8 + n_src], refs[9 + n_src:9 + n_src + n_dst]
    halo_ref, wsm_ref, y_ref = refs[9 + n_src + n_dst:]
    _cast_slabs(cast_src, cast_dst, scaled)
    i = pl.program_id(0)
    seq_tile = i % tiles_per_seq
    group = d_pool // len(POOL_WINDOWS)
    head = d_sgu // N_SGU_HEADS

    @pl.when(i == 0)
    def _():
        t = lax.broadcasted_iota(jnp.int32, (SGU_BLOCK, SGU_BLOCK), 0)
        s = lax.broadcasted_iota(jnp.int32, (SGU_BLOCK, SGU_BLOCK), 1)
        mask = (s // CHUNK) <= (t // CHUNK)
        for h in range(N_SGU_HEADS):
            wsm_ref[h] = jnp.where(mask, ws_ref[h], 0.0).astype(BF16)

    @pl.when(seq_tile == 0)
    def _():
        halo_ref[...] = jnp.zeros_like(halo_ref)

    x = x_ref[...]
    xn = _normalize(x).astype(BF16)
    a = _dot(xn, w_in_ref[:, 0:d_pool])
    u = _dot(xn, w_in_ref[:, d_pool:d_pool + d_sgu])
    v = _dot(xn, w_in_ref[:, d_pool + d_sgu:d_pool + 2 * d_sgu])

    ext = jnp.concatenate([halo_ref[...], a], axis=0)
    halo_ref[...] = a[tm - HALO:, :]
    pos = (seq_tile * tm + lax.broadcasted_iota(jnp.int32, (tm, 1), 0) + 1).astype(F32)
    for gi, w in enumerate(POOL_WINDOWS):
        cols = slice(gi * group, (gi + 1) * group)
        s = ext[:, cols]
        k = 1
        while k < w:
            s = s + pltpu.roll(s, k, axis=0)
            k *= 2
        inv = 1.0 / jnp.minimum(pos, float(w))
        p = (s[HALO:] * inv - a[:, cols]).astype(BF16)
        y_ref[:, cols] = (_dot(p, pool_w_ref[gi]) * pool_scale_ref[:, cols]).astype(BF16)

    vn = _rms(v, sgu_g_ref[...]).astype(BF16)
    for blk in range(tm // SGU_BLOCK):
        rows = slice(blk * SGU_BLOCK, (blk + 1) * SGU_BLOCK)
        for h in range(N_SGU_HEADS):
            cols = slice(h * head, (h + 1) * head)
            mixed = _dot(wsm_ref[h], vn[rows, cols]) + bias_ref[:, cols]
            y_ref[rows, d_pool + h * head:d_pool + (h + 1) * head] = (u[rows, cols] * mixed).astype(BF16)

    o_ref[...] = x + _dot(y_ref[...], w_out_ref[...])


def _mix_layer(x2, w_in, pool_w, pool_scale, sgu_g, ws, bias, w_out, cast_jobs, *, seq, tm=512):
    rows, d = x2.shape
    d_in = w_in.shape[1]
    d_mix = w_out.shape[0]
    d_pool = pool_scale.shape[1]
    d_sgu = d_mix - d_pool
    group = d_pool // len(POOL_WINDOWS)
    cast_ops, cast_in, cast_out, cast_shapes = _cast_specs(cast_jobs, rows // tm)
    kern = functools.partial(_mix_kernel, scaled=tuple(s is not None for _, s in cast_jobs), tm=tm,
                             tiles_per_seq=seq // tm, d_pool=d_pool, d_sgu=d_sgu)
    return pl.pallas_call(
        kern,
        grid=(rows // tm,),
        in_specs=[
            pl.BlockSpec((tm, d), lambda i: (i, 0)),
            _resident((d, d_in)),
            _resident((len(POOL_WINDOWS), group, group)),
            _resident((1, d_pool)),
            _resident((1, d_sgu)),
            _resident((N_SGU_HEADS, SGU_BLOCK, SGU_BLOCK)),
            _resident((SGU_BLOCK, d_sgu)),
            _resident((d_mix, d)),
            *cast_in,
        ],
        out_specs=[pl.BlockSpec((tm, d), lambda i: (i, 0)), *cast_out],
        out_shape=[jax.ShapeDtypeStruct((rows, d), F32), *cast_shapes],
        scratch_shapes=[
            pltpu.VMEM((HALO, d_pool), F32),
            pltpu.VMEM((N_SGU_HEADS, SGU_BLOCK, SGU_BLOCK), BF16),
            pltpu.VMEM((tm, d_mix), BF16),
        ],
        compiler_params=pltpu.CompilerParams(
            dimension_semantics=("arbitrary",), vmem_limit_bytes=VMEM_LIMIT),
        name="mix_layer",
    )(x2, w_in, pool_w, pool_scale, sgu_g, ws, bias, w_out, *cast_ops)


def _kv_kernel(mem_ref, g_ref, wk_ref, wv_ref, wq_ref, wo_ref, qk_ref, vo_ref, m_ref, *, batch, n_mem):
    @pl.when(pl.program_id(0) == 0)
    def _():
        m_ref[...] = _rms(mem_ref[...], g_ref[...]).astype(BF16)

    m = m_ref[...]
    k = _dot(m, wk_ref[...]).astype(BF16)
    v = _dot(m, wv_ref[...]).astype(BF16)
    scale = wq_ref.shape[1] ** -0.5
    for b in range(batch):
        rows = slice(b * n_mem, (b + 1) * n_mem)
        qk = lax.dot_general(wq_ref[...], k[rows], (((1,), (1,)), ((), ())), preferred_element_type=F32)
        qk_ref[b] = (qk * scale).astype(BF16)
        vo_ref[b] = _dot(v[rows], wo_ref[...]).astype(BF16)


def _kv_proj(mem2, g, wk, wv, wq, wo, *, batch, n_mem):
    rows, d = mem2.shape
    hd = d // N_XATTN_HEADS
    kern = functools.partial(_kv_kernel, batch=batch, n_mem=n_mem)
    return pl.pallas_call(
        kern,
        grid=(N_XATTN_HEADS,),
        in_specs=[
            _resident((rows, d)),
            _resident((1, d)),
            pl.BlockSpec((d, hd), lambda h: (0, h)),
            pl.BlockSpec((d, hd), lambda h: (0, h)),
            pl.BlockSpec((d, hd), lambda h: (0, h)),
            pl.BlockSpec((hd, d), lambda h: (h, 0)),
        ],
        out_specs=[
            pl.BlockSpec((batch, d, n_mem), lambda h: (0, 0, h)),
            pl.BlockSpec((batch, n_mem, d), lambda h: (0, h, 0)),
        ],
        out_shape=[
            jax.ShapeDtypeStruct((batch, d, N_XATTN_HEADS * n_mem), BF16),
            jax.ShapeDtypeStruct((batch, N_XATTN_HEADS * n_mem, d), BF16),
        ],
        scratch_shapes=[pltpu.VMEM((rows, d), BF16)],
        compiler_params=pltpu.CompilerParams(
            dimension_semantics=("arbitrary",), vmem_limit_bytes=VMEM_LIMIT),
        name="kv_proj",
    )(mem2, g, wk, wv, wq, wo)


def _attn_kernel(*refs, scaled, n_mem):
    x_ref, qk_ref, vo_ref = refs[:3]
    n_src = len(scaled) + sum(scaled)
    cast_src, o_ref, cast_dst = refs[3:3 + n_src], refs[3 + n_src], refs[4 + n_src:]
    _cast_slabs(cast_src, cast_dst, scaled)
    x = x_ref[...]
    h = _normalize(x).astype(BF16)
    s = _dot(h, qk_ref[0])
    ps = []
    for hd in range(N_XATTN_HEADS):
        sh = s[:, hd * n_mem:(hd + 1) * n_mem]
        e = jnp.exp(sh - jnp.max(sh, axis=-1, keepdims=True))
        ps.append((e * (1.0 / jnp.sum(e, axis=-1, keepdims=True))).astype(BF16))
    p = jnp.concatenate(ps, axis=1)
    o_ref[...] = x + _dot(p, vo_ref[0])


def _attn_layer(x2, qk, vo, cast_jobs, *, seq, tm=512):
    rows, d = x2.shape
    tiles_per_seq = seq // tm
    hm = qk.shape[2]
    cast_ops, cast_in, cast_out, cast_shapes = _cast_specs(cast_jobs, rows // tm)
    kern = functools.partial(_attn_kernel, scaled=tuple(s is not None for _, s in cast_jobs),
                             n_mem=hm // N_XATTN_HEADS)
    return pl.pallas_call(
        kern,
        grid=(rows // tm,),
        in_specs=[
            pl.BlockSpec((tm, d), lambda i: (i, 0)),
            pl.BlockSpec((1, d, hm), lambda i: (i // tiles_per_seq, 0, 0)),
            pl.BlockSpec((1, hm, d), lambda i: (i // tiles_per_seq, 0, 0)),
            *cast_in,
        ],
        out_specs=[pl.BlockSpec((tm, d), lambda i: (i, 0)), *cast_out],
        out_shape=[jax.ShapeDtypeStruct((rows, d), F32), *cast_shapes],
        compiler_params=pltpu.CompilerParams(
            dimension_semantics=("arbitrary",), vmem_limit_bytes=VMEM_LIMIT),
        name="attn_layer",
    )(x2, qk, vo, *cast_ops)


def _ffn_kernel(x_ref, wg_ref, wu_ref, wd_ref, fg_ref, o_ref, h_ref, *, final_norm):
    j = pl.program_id(1)

    @pl.when(j == 0)
    def _():
        x = x_ref[...]
        h_ref[...] = _normalize(x).astype(BF16)
        o_ref[...] = x

    h = h_ref[...]
    gate = _dot(h, wg_ref[...])
    up = _dot(h, wu_ref[...])
    act = (gate * jax.nn.sigmoid(gate) * up).astype(BF16)
    o_ref[...] += _dot(act, wd_ref[...])

    if final_norm:
        @pl.when(j == pl.num_programs(1) - 1)
        def _():
            o_ref[...] = _rms(o_ref[...], fg_ref[...])


def _ffn_layer(x2, wg, wu, wd, fg, *, final_norm, tm=1024, tf=512):
    rows, d = x2.shape
    d_ff = wg.shape[1]
    kern = functools.partial(_ffn_kernel, final_norm=final_norm)
    return pl.pallas_call(
        kern,
        grid=(rows // tm, d_ff // tf),
        in_specs=[
            pl.BlockSpec((tm, d), lambda i, j: (i, 0)),
            pl.BlockSpec((d, tf), lambda i, j: (0, j)),
            pl.BlockSpec((d, tf), lambda i, j: (0, j)),
            pl.BlockSpec((tf, d), lambda i, j: (j, 0)),
            _resident((1, d)),
        ],
        out_specs=pl.BlockSpec((tm, d), lambda i, j: (i, 0)),
        out_shape=jax.ShapeDtypeStruct((rows, d), F32),
        scratch_shapes=[pltpu.VMEM((tm, d), BF16)],
        compiler_params=pltpu.CompilerParams(
            dimension_semantics=("arbitrary", "arbitrary"), vmem_limit_bytes=VMEM_LIMIT),
        name="ffn_layer",
    )(x2, wg, wu, wd, fg)


def kernel(x, mem, norm_mix_g, w_in, pool_w, pool_scale, sgu_norm_g, w_spatial, b_spatial, w_out,
           norm_xattn_g, norm_mem_g, w_q, w_k, w_v, w_o, norm_ffn_g, w_gate, w_up, w_down, final_norm_g):
    batch, seq, d = x.shape
    n_mem = mem.shape[1]
    depth = w_in.shape[0]
    assert depth >= 1, "the final norm is fused into the last layer"
    d_sgu = sgu_norm_g.shape[1]
    head = d_sgu // N_SGU_HEADS

    x2 = x.reshape(batch * seq, d)
    mem2 = mem.reshape(batch * n_mem, d)
    row = lambda a: a.reshape(1, -1)
    col = lambda a: a.reshape(-1, 1)
    for l in range(depth):
        bias = jnp.repeat(b_spatial[l].T, head, axis=1)
        w_in_g = (col(norm_mix_g[l]) * w_in[l]).astype(BF16)
        x2, wk, wv, wq, wo, wg, wu = _mix_layer(
            x2, w_in_g, pool_w[l].astype(BF16), row(pool_scale[l]), row(sgu_norm_g[l]), w_spatial[l], bias,
            w_out[l].astype(BF16),
            ((w_k[l], None), (w_v[l], None), (w_q[l], col(norm_xattn_g[l])), (w_o[l], None),
             (w_gate[l], col(norm_ffn_g[l])), (w_up[l], col(norm_ffn_g[l]))), seq=seq)
        qk, vo = _kv_proj(mem2, row(norm_mem_g[l]), wk, wv, wq, wo, batch=batch, n_mem=n_mem)
        x2, wd = _attn_layer(x2, qk, vo, ((w_down[l], None),), seq=seq)
        x2 = _ffn_layer(x2, wg, wu, wd, row(final_norm_g), final_norm=(l == depth - 1))
    return x2.reshape(batch, seq, d)
```

```python
import functools

import jax
import jax.numpy as jnp
from jax import lax
from jax.experimental import pallas as pl
from jax.experimental.pallas import tpu as pltpu

EPS = 1e-6
CHUNK = 64
POOL_WINDOWS = (2, 4, 8, 16)
SGU_BLOCK = 128
N_SGU_HEADS = 8
N_XATTN_HEADS = 4

HALO = 16
BF16_SUBLANES = 16
VMEM_LIMIT = 62 * 1024 * 1024

BF16 = jnp.bfloat16
F32 = jnp.float32


def _normalize(x):
    ms = jnp.mean(x * x, axis=-1, keepdims=True)
    return x * lax.rsqrt(ms + EPS)


def _rms(x, g):
    return _normalize(x) * g


def _dot(a, b):
    return jnp.dot(a, b, preferred_element_type=F32)


def _resident(shape):
    return pl.BlockSpec(shape, lambda *_: (0,) * len(shape), pipeline_mode=pl.Buffered(1))


def _cast_specs(jobs, n_steps):
    operands, in_specs, out_specs, out_shapes = [], [], [], []
    for w, scale in jobs:
        slab = w.shape[0] // n_steps
        assert slab * n_steps == w.shape[0] and slab % BF16_SUBLANES == 0, (w.shape, n_steps)
        operands.append(w)
        in_specs.append(pl.BlockSpec((slab, w.shape[1]), lambda i: (i, 0)))
        if scale is not None:
            operands.append(scale)
            in_specs.append(pl.BlockSpec((slab, 1), lambda i: (i, 0)))
        out_specs.append(pl.BlockSpec((slab, w.shape[1]), lambda i: (i, 0)))
        out_shapes.append(jax.ShapeDtypeStruct(w.shape, BF16))
    return operands, in_specs, out_specs, out_shapes


def _cast_slabs(src_refs, dst_refs, scaled):
    src = iter(src_refs)
    for dst, has_scale in zip(dst_refs, scaled):
        w = next(src)[...]
        if has_scale:
            w = w * next(src)[...]
        dst[...] = w.astype(BF16)


def _mix_kernel(*refs, scaled, tm, tiles_per_seq, d_pool, d_sgu):
    (x_ref, w_in_ref, pool_w_ref, pool_scale_ref, sgu_g_ref, ws_ref, bias_ref, w_out_ref) = refs[:8]
    n_src, n_dst = len(scaled) + sum(scaled), len(scaled)
    cast_src, o_ref, cast_dst = refs[8:8 + n_src], refs[8 + n_src], refs[9 + n_src:9 + n_src + n_dst]
    halo_ref, wsm_ref, y_ref = refs[9 + n_src + n_dst:]
    _cast_slabs(cast_src, cast_dst, scaled)
    i = pl.program_id(0)
    seq_tile = i % tiles_per_seq
    group = d_pool // len(POOL_WINDOWS)
    head = d_sgu // N_SGU_HEADS

    @pl.when(i == 0)
    def _():
        t = lax.broadcasted_iota(jnp.int32, (SGU_BLOCK, SGU_BLOCK), 0)
        s = lax.broadcasted_iota(jnp.int32, (SGU_BLOCK, SGU_BLOCK), 1)
        mask = (s // CHUNK) <= (t // CHUNK)
        for h in range(N_SGU_HEADS):
            wsm_ref[h] = jnp.where(mask, ws_ref[h], 0.0).astype(BF16)

    @pl.when(seq_tile == 0)
    def _():
        halo_ref[...] = jnp.zeros_like(halo_ref)

    x = x_ref[...]
    xn = _normalize(x).astype(BF16)
    a = _dot(xn, w_in_ref[:, 0:d_pool])
    u = _dot(xn, w_in_ref[:, d_pool:d_pool + d_sgu])
    v = _dot(xn, w_in_ref[:, d_pool + d_sgu:d_pool + 2 * d_sgu])

    ext = jnp.concatenate([halo_ref[...], a], axis=0)
    halo_ref[...] = a[tm - HALO:, :]
    pos = (seq_tile * tm + lax.broadcasted_iota(jnp.int32, (tm, 1), 0) + 1).astype(F32)
    for gi, w in enumerate(POOL_WINDOWS):
        cols = slice(gi * group, (gi + 1) * group)
        s = ext[:, cols]
        k = 1
        while k < w:
            s = s + pltpu.roll(s, k, axis=0)
            k *= 2
        inv = 1.0 / jnp.minimum(pos, float(w))
        p = (s[HALO:] * inv - a[:, cols]).astype(BF16)
        y_ref[:, cols] = (_dot(p, pool_w_ref[gi]) * pool_scale_ref[:, cols]).astype(BF16)

    vn = _rms(v, sgu_g_ref[...]).astype(BF16)
    for blk in range(tm // SGU_BLOCK):
        rows = slice(blk * SGU_BLOCK, (blk + 1) * SGU_BLOCK)
        for h in range(N_SGU_HEADS):
            cols = slice(h * head, (h + 1) * head)
            mixed = _dot(wsm_ref[h], vn[rows, cols]) + bias_ref[:, cols]
            y_ref[rows, d_pool + h * head:d_pool + (h + 1) * head] = (u[rows, cols] * mixed).astype(BF16)

    o_ref[...] = x + _dot(y_ref[...], w_out_ref[...])


def _mix_layer(x2, w_in, pool_w, pool_scale, sgu_g, ws, bias, w_out, cast_jobs, *, seq, tm=512):
    rows, d = x2.shape
    d_in = w_in.shape[1]
    d_mix = w_out.shape[0]
    d_pool = pool_scale.shape[1]
    d_sgu = d_mix - d_pool
    group = d_pool // len(POOL_WINDOWS)
    cast_ops, cast_in, cast_out, cast_shapes = _cast_specs(cast_jobs, rows // tm)
    kern = functools.partial(_mix_kernel, scaled=tuple(s is not None for _, s in cast_jobs), tm=tm,
                             tiles_per_seq=seq // tm, d_pool=d_pool, d_sgu=d_sgu)
    return pl.pallas_call(
        kern,
        grid=(rows // tm,),
        in_specs=[
            pl.BlockSpec((tm, d), lambda i: (i, 0)),
            _resident((d, d_in)),
            _resident((len(POOL_WINDOWS), group, group)),
            _resident((1, d_pool)),
            _resident((1, d_sgu)),
            _resident((N_SGU_HEADS, SGU_BLOCK, SGU_BLOCK)),
            _resident((SGU_BLOCK, d_sgu)),
            _resident((d_mix, d)),
            *cast_in,
        ],
        out_specs=[pl.BlockSpec((tm, d), lambda i: (i, 0)), *cast_out],
        out_shape=[jax.ShapeDtypeStruct((rows, d), F32), *cast_shapes],
        scratch_shapes=[
            pltpu.VMEM((HALO, d_pool), F32),
            pltpu.VMEM((N_SGU_HEADS, SGU_BLOCK, SGU_BLOCK), BF16),
            pltpu.VMEM((tm, d_mix), BF16),
        ],
        compiler_params=pltpu.CompilerParams(
            dimension_semantics=("arbitrary",), vmem_limit_bytes=VMEM_LIMIT),
        name="mix_layer",
    )(x2, w_in, pool_w, pool_scale, sgu_g, ws, bias, w_out, *cast_ops)


def _kv_kernel(mem_ref, g_ref, wk_ref, wv_ref, wq_ref, wo_ref, qk_ref, vo_ref, m_ref, *, batch, n_mem):
    @pl.when(pl.program_id(0) == 0)
    def _():
        m_ref[...] = _rms(mem_ref[...], g_ref[...]).astype(BF16)

    m = m_ref[...]
    k = _dot(m, wk_ref[...]).astype(BF16)
    v = _dot(m, wv_ref[...]).astype(BF16)
    scale = wq_ref.shape[1] ** -0.5
    for b in range(batch):
        rows = slice(b * n_mem, (b + 1) * n_mem)
        qk = lax.dot_general(wq_ref[...], k[rows], (((1,), (1,)), ((), ())), preferred_element_type=F32)
        qk_ref[b] = (qk * scale).astype(BF16)
        vo_ref[b] = _dot(v[rows], wo_ref[...]).astype(BF16)


def _kv_proj(mem2, g, wk, wv, wq, wo, *, batch, n_mem):
    rows, d = mem2.shape
    hd = d // N_XATTN_HEADS
    kern = functools.partial(_kv_kernel, batch=batch, n_mem=n_mem)
    return pl.pallas_call(
        kern,
        grid=(N_XATTN_HEADS,),
        in_specs=[
            _resident((rows, d)),
            _resident((1, d)),
            pl.BlockSpec((d, hd), lambda h: (0, h)),
            pl.BlockSpec((d, hd), lambda h: (0, h)),
            pl.BlockSpec((d, hd), lambda h: (0, h)),
            pl.BlockSpec((hd, d), lambda h: (h, 0)),
        ],
        out_specs=[
            pl.BlockSpec((batch, d, n_mem), lambda h: (0, 0, h)),
            pl.BlockSpec((batch, n_mem, d), lambda h: (0, h, 0)),
        ],
        out_shape=[
            jax.ShapeDtypeStruct((batch, d, N_XATTN_HEADS * n_mem), BF16),
            jax.ShapeDtypeStruct((batch, N_XATTN_HEADS * n_mem, d), BF16),
        ],
        scratch_shapes=[pltpu.VMEM((rows, d), BF16)],
        compiler_params=pltpu.CompilerParams(
            dimension_semantics=("arbitrary",), vmem_limit_bytes=VMEM_LIMIT),
        name="kv_proj",
    )(mem2, g, wk, wv, wq, wo)


def _attn_kernel(*refs, scaled, n_mem):
    x_ref, qk_ref, vo_ref = refs[:3]
    n_src = len(scaled) + sum(scaled)
    cast_src, o_ref, cast_dst = refs[3:3 + n_src], refs[3 + n_src], refs[4 + n_src:]
    _cast_slabs(cast_src, cast_dst, scaled)
    x = x_ref[...]
    h = _normalize(x).astype(BF16)
    s = _dot(h, qk_ref[0])
    ps = []
    for hd in range(N_XATTN_HEADS):
        sh = s[:, hd * n_mem:(hd + 1) * n_mem]
        e = jnp.exp(sh - jnp.max(sh, axis=-1, keepdims=True))
        ps.append((e * (1.0 / jnp.sum(e, axis=-1, keepdims=True))).astype(BF16))
    p = jnp.concatenate(ps, axis=1)
    o_ref[...] = x + _dot(p, vo_ref[0])


def _attn_layer(x2, qk, vo, cast_jobs, *, seq, tm=512):
    rows, d = x2.shape
    tiles_per_seq = seq // tm
    hm = qk.shape[2]
    cast_ops, cast_in, cast_out, cast_shapes = _cast_specs(cast_jobs, rows // tm)
    kern = functools.partial(_attn_kernel, scaled=tuple(s is not None for _, s in cast_jobs),
                             n_mem=hm // N_XATTN_HEADS)
    return pl.pallas_call(
        kern,
        grid=(rows // tm,),
        in_specs=[
            pl.BlockSpec((tm, d), lambda i: (i, 0)),
            pl.BlockSpec((1, d, hm), lambda i: (i // tiles_per_seq, 0, 0)),
            pl.BlockSpec((1, hm, d), lambda i: (i // tiles_per_seq, 0, 0)),
            *cast_in,
        ],
        out_specs=[pl.BlockSpec((tm, d), lambda i: (i, 0)), *cast_out],
        out_shape=[jax.ShapeDtypeStruct((rows, d), F32), *cast_shapes],
        compiler_params=pltpu.CompilerParams(
            dimension_semantics=("arbitrary",), vmem_limit_bytes=VMEM_LIMIT),
        name="attn_layer",
    )(x2, qk, vo, *cast_ops)


def _ffn_kernel(x_hbm, wg_a, wu_a, wd_a, wg_b, wu_b, wd_b, fg_ref, o_ref, x_buf, h_ref, x_sem, *,
                tm, n_chunks, final_norm):
    i, j = pl.program_id(0), pl.program_id(1)
    n_i, n_j = pl.num_programs(0), pl.num_programs(1)

    def x_copy(tile):
        return pltpu.make_async_copy(x_hbm.at[pl.ds(tile * tm, tm), :], x_buf, x_sem)

    @pl.when(jnp.logical_and(i == 0, j == 0))
    def _():
        x_copy(0).start()

    @pl.when(j == 0)
    def _():
        x_copy(i).wait()
        x = x_buf[...]
        h_ref[...] = _normalize(x).astype(BF16)
        o_ref[...] = x

    @pl.when(jnp.logical_and(j == 0, i + 1 < n_i))
    def _():
        x_copy(i + 1).start()

    def chunk(wg_ref, wu_ref, wd_ref):
        h = h_ref[...]
        gate = _dot(h, wg_ref[...])
        up = _dot(h, wu_ref[...])
        act = (gate * jax.nn.sigmoid(gate) * up).astype(BF16)
        return _dot(act, wd_ref[...])

    has_pair = 2 * j + 1 < n_chunks

    @pl.when(has_pair)
    def _():
        o_ref[...] += chunk(wg_a, wu_a, wd_a)
        o_ref[...] += chunk(wg_b, wu_b, wd_b)

    @pl.when(jnp.logical_not(has_pair))
    def _():
        o_ref[...] += chunk(wg_a, wu_a, wd_a)

    if final_norm:
        @pl.when(j == n_j - 1)
        def _():
            o_ref[...] = _rms(o_ref[...], fg_ref[...])


def _ffn_layer(x2, wg, wu, wd, fg, *, final_norm, tm=1024, tf=512):
    rows, d = x2.shape
    d_ff = wg.shape[1]
    n_chunks = d_ff // tf
    last = n_chunks - 1
    kern = functools.partial(_ffn_kernel, tm=tm, n_chunks=n_chunks, final_norm=final_norm)
    col_a = lambda i, j: (0, 2 * j)
    col_b = lambda i, j: (0, jnp.minimum(2 * j + 1, last))
    row_a = lambda i, j: (2 * j, 0)
    row_b = lambda i, j: (jnp.minimum(2 * j + 1, last), 0)
    return pl.pallas_call(
        kern,
        grid=(rows // tm, (n_chunks + 1) // 2),
        in_specs=[
            pl.BlockSpec(memory_space=pl.ANY),
            pl.BlockSpec((d, tf), col_a),
            pl.BlockSpec((d, tf), col_a),
            pl.BlockSpec((tf, d), row_a),
            pl.BlockSpec((d, tf), col_b),
            pl.BlockSpec((d, tf), col_b),
            pl.BlockSpec((tf, d), row_b),
            _resident((1, d)),
        ],
        out_specs=pl.BlockSpec((tm, d), lambda i, j: (i, 0)),
        out_shape=jax.ShapeDtypeStruct((rows, d), F32),
        scratch_shapes=[
            pltpu.VMEM((tm, d), F32),
            pltpu.VMEM((tm, d), BF16),
            pltpu.SemaphoreType.DMA,
        ],
        compiler_params=pltpu.CompilerParams(
            dimension_semantics=("arbitrary", "arbitrary"), vmem_limit_bytes=VMEM_LIMIT),
        name="ffn_layer",
    )(x2, wg, wu, wd, wg, wu, wd, fg)


def kernel(x, mem, norm_mix_g, w_in, pool_w, pool_scale, sgu_norm_g, w_spatial, b_spatial, w_out,
           norm_xattn_g, norm_mem_g, w_q, w_k, w_v, w_o, norm_ffn_g, w_gate, w_up, w_down, final_norm_g):
    batch, seq, d = x.shape
    n_mem = mem.shape[1]
    depth = w_in.shape[0]
    assert depth >= 1, "the final norm is fused into the last layer"
    d_sgu = sgu_norm_g.shape[1]
    head = d_sgu // N_SGU_HEADS

    x2 = x.reshape(batch * seq, d)
    mem2 = mem.reshape(batch * n_mem, d)
    row = lambda a: a.reshape(1, -1)
    col = lambda a: a.reshape(-1, 1)
    for l in range(depth):
        bias = jnp.repeat(b_spatial[l].T, head, axis=1)
        w_in_g = (col(norm_mix_g[l]) * w_in[l]).astype(BF16)
        x2, wk, wv, wq, wo, wg, wu = _mix_layer(
            x2, w_in_g, pool_w[l].astype(BF16), row(pool_scale[l]), row(sgu_norm_g[l]), w_spatial[l], bias,
            w_out[l].astype(BF16),
            ((w_k[l], None), (w_v[l], None), (w_q[l], col(norm_xattn_g[l])), (w_o[l], None),
             (w_gate[l], col(norm_ffn_g[l])), (w_up[l], col(norm_ffn_g[l]))), seq=seq)
        qk, vo = _kv_proj(mem2, row(norm_mem_g[l]), wk, wv, wq, wo, batch=batch, n_mem=n_mem)
        x2, wd = _attn_layer(x2, qk, vo, ((w_down[l], None),), seq=seq)
        x2 = _ffn_layer(x2, wg, wu, wd, row(final_norm_g), final_norm=(l == depth - 1))
    return x2.reshape(batch, seq, d)
```
